```python
import math
import jax
import jax.numpy as jnp
from jax import lax
import numpy as np

D_MODEL = 1024
BATCH = 4
SEQ = 4096
DEPTH = 2
DEC_BATCH = 32
DEC_SEQ = 8
PAST_LEN = 16384
PAGE_SIZE = 128

N_HEADS = 8
DK = 64
DV = 2 * DK
W_A = N_HEADS * DV
W_B = D_MODEL
W_C = D_MODEL
N_BRANCH = 3
CONV_B = 31
CONV_C = 3
CONV_F = 3
D_FF = 2816
Q_BLOCK = 128
EPS = 1e-5
ALPHA = (2 * DEPTH) ** 0.25
BETA = (8 * DEPTH) ** -0.25
QK_W = N_HEADS * 2 * DK
SPLITS = (QK_W, 2 * QK_W, 2 * QK_W + W_A, 2 * QK_W + W_A + 2 * W_B, 2 * QK_W + W_A + 2 * W_B + 3 * W_C)
N_IN = 2 * QK_W + W_A + 2 * W_B + 3 * W_C + N_BRANCH * D_MODEL

kernel_name = "hybrid_diffattn_conformer_shortconv_step"


def alibi_slopes():
    return jnp.exp2(-8.0 * jnp.arange(1, N_HEADS + 1, dtype=jnp.float32) / N_HEADS)


def lambda_init(layer):
    return 0.8 - 0.6 * math.exp(-0.3 * layer)


def layer_norm(x, g, b):
    xf = x.astype(jnp.float32)
    mu = jnp.mean(xf, axis=-1, keepdims=True)
    var = jnp.mean(jnp.square(xf - mu), axis=-1, keepdims=True)
    return ((xf - mu) * lax.rsqrt(var + EPS) * g.astype(jnp.float32) + b.astype(jnp.float32)).astype(x.dtype)


def rms_norm(x, g):
    xf = x.astype(jnp.float32)
    return (xf * lax.rsqrt(jnp.mean(jnp.square(xf), axis=-1, keepdims=True) + EPS) * g.astype(jnp.float32)).astype(x.dtype)


def causal_dwconv(u, prev, w):
    padded = jnp.concatenate([prev.astype(u.dtype), u], axis=1)
    out = lax.conv_general_dilated(padded, w[:, None, :].astype(u.dtype), window_strides=(1,), padding='VALID',
                                   dimension_numbers=('NWC', 'WIO', 'NWC'), feature_group_count=u.shape[-1])
    return out, padded[:, padded.shape[1] - (w.shape[0] - 1):]


def diff_attn_prompt(q1, q2, k1, k2, v, lam, slopes):
    nb, s_len, h, _ = q1.shape
    n_blk = s_len // Q_BLOCK
    scale = DK ** -0.5
    kpos = jnp.arange(s_len)

    def to_blocks(a):
        return jnp.moveaxis(a.reshape(nb, n_blk, Q_BLOCK, h, a.shape[-1]), 1, 0)

    def one_block(args):
        i, qa, qb = args
        qpos = i * Q_BLOCK + jnp.arange(Q_BLOCK)
        dist = (qpos[:, None] - kpos[None, :]).astype(jnp.float32)
        bias = -slopes[:, None, None] * dist
        mask = dist >= 0

        def probs(qq, kk):
            sc = jnp.einsum('bqhd,bkhd->bhqk', qq, kk).astype(jnp.float32) * scale + bias
            return jax.nn.softmax(jnp.where(mask, sc, -jnp.inf), axis=-1)

        p = probs(qa, k1) - lam * probs(qb, k2)
        return jnp.einsum('bhqk,bkhd->bqhd', p.astype(v.dtype), v)

    out = lax.map(one_block, (jnp.arange(n_blk), to_blocks(q1), to_blocks(q2)))
    return jnp.moveaxis(out, 0, 1).reshape(nb, s_len, h, v.shape[-1])


def diff_attn_sample(q1, q2, k1, k2, v, lam, pool_k, pool_v, page_table, layer, slopes):
    nd, t, _, _ = q1.shape
    n_pages = page_table.shape[1]
    scale = DK ** -0.5
    qpos = PAST_LEN + jnp.arange(t)

    def masked_scores(qq, kk, kpos):
        dist = (qpos[:, None] - kpos[None, :]).astype(jnp.float32)
        sc = jnp.einsum('bqhd,bkhd->bhqk', qq, kk).astype(jnp.float32) * scale - slopes[:, None, None] * dist
        return jnp.where(dist >= 0, sc, -jnp.inf)

    def start(qq, kk):
        sc = masked_scores(qq, kk, qpos)
        m = jnp.max(sc, axis=-1)
        p = jnp.exp(sc - m[..., None])
        return m, jnp.sum(p, axis=-1), jnp.einsum('bhqk,bkhd->bhqd', p, v.astype(jnp.float32))

    def update(m, l_sum, acc, qq, kk, vb, kpos):
        sc = masked_scores(qq, kk, kpos)
        m_new = jnp.maximum(m, jnp.max(sc, axis=-1))
        c = jnp.exp(m - m_new)
        p = jnp.exp(sc - m_new[..., None])
        return m_new, l_sum * c + jnp.sum(p, axis=-1), acc * c[..., None] + jnp.einsum('bhqk,bkhd->bhqd', p, vb)

    def step(carry, j):
        m1, l1, a1, m2, l2, a2 = carry
        pages = page_table[:, j]
        kb = pool_k[layer, pages]
        vb = pool_v[layer, pages].astype(jnp.float32)
        kpos = j * PAGE_SIZE + jnp.arange(PAGE_SIZE)
        m1, l1, a1 = update(m1, l1, a1, q1, kb[..., :DK], vb, kpos)
        m2, l2, a2 = update(m2, l2, a2, q2, kb[..., DK:], vb, kpos)
        return (m1, l1, a1, m2, l2, a2), None

    init = start(q1, k1) + start(q2, k2)
    (m1, l1, a1, m2, l2, a2), _ = lax.scan(step, init, jnp.arange(n_pages))
    o = a1 / l1[..., None] - lam * (a2 / l2[..., None])
    return jnp.transpose(o, (0, 2, 1, 3)).astype(v.dtype)


def trunk_layer(x, layer, attend, prev_b, prev_c, prev_f, w_in, lq1, lk1, lq2, lk2, attn_g, cb_w, cb_bias, cb_ng, cb_nb,
                cc_w, w_br, w_o, ln1_g, ln1_b, w_up, cf_w, cf_bias, w_down, ln2_g, ln2_b):
    nb, t, _ = x.shape
    lam_init = lambda_init(layer)
    h = x @ w_in
    q, k, v, u_b, u_c, u_g = jnp.split(h, SPLITS, axis=-1)
    q = q.reshape(nb, t, N_HEADS, 2, DK)
    k = k.reshape(nb, t, N_HEADS, 2, DK)
    v = v.reshape(nb, t, N_HEADS, DV)
    f32 = jnp.float32
    lam = (jnp.exp(jnp.sum(lq1.astype(f32) * lk1.astype(f32))) - jnp.exp(jnp.sum(lq2.astype(f32) * lk2.astype(f32)))
           + lam_init)
    o_a = attend(q[..., 0, :], q[..., 1, :], k[..., 0, :], k[..., 1, :], v, lam)
    o_a = (rms_norm(o_a, attn_g) * (1.0 - lam_init)).reshape(nb, t, W_A)
    a_b, g_b = jnp.split(u_b, 2, axis=-1)
    c_b, new_b = causal_dwconv(a_b * jax.nn.sigmoid(g_b), prev_b, cb_w)
    o_b = jax.nn.silu(layer_norm(c_b + cb_bias, cb_ng, cb_nb))
    gate_b, gate_c, h_c = jnp.split(u_c, 3, axis=-1)
    c_c, new_c = causal_dwconv(gate_c * h_c, prev_c, cc_w)
    o_c = gate_b * c_c
    proj = jnp.einsum('nbtc,ncd->nbtd', jnp.stack([o_a, o_b, o_c]), w_br)
    gates = jax.nn.sigmoid(u_g.reshape(nb, t, N_BRANCH, D_MODEL))
    merged = jnp.einsum('btnd,nbtd->btd', gates, proj)
    x = layer_norm(ALPHA * x + merged @ w_o, ln1_g, ln1_b)
    u, g_f = jnp.split(x @ w_up, 2, axis=-1)
    c_f, new_f = causal_dwconv(g_f, prev_f, cf_w)
    x = layer_norm(ALPHA * x + (jax.nn.silu(c_f + cf_bias) * u) @ w_down, ln2_g, ln2_b)
    return x, k.reshape(nb, t, N_HEADS, 2 * DK), v, new_b, new_c, new_f


def setup_inputs(seed: int = 0) -> dict:
    key = jax.random.key(seed)
    ks = jax.random.split(key, 32)
    n_pages = PAST_LEN // PAGE_SIZE
    n_pool = (DEC_BATCH * n_pages * 5) // 4

    def nrm(k, shape, scale=1.0):
        return jax.random.normal(k, shape, jnp.float32) * scale

    x_prompt = nrm(ks[0], (BATCH, SEQ, D_MODEL))
    x_sample = nrm(ks[1], (DEC_BATCH, DEC_SEQ, D_MODEL))
    cache_k = nrm(ks[2], (DEPTH, n_pool, PAGE_SIZE, N_HEADS, 2 * DK))
    cache_v = nrm(ks[3], (DEPTH, n_pool, PAGE_SIZE, N_HEADS, DV))
    state_conv_b = nrm(ks[4], (DEPTH, DEC_BATCH, CONV_B - 1, W_B))
    state_conv_c = nrm(ks[5], (DEPTH, DEC_BATCH, CONV_C - 1, W_C))
    state_conv_ffn = nrm(ks[6], (DEPTH, DEC_BATCH, CONV_F - 1, D_FF))
    page_table = jax.random.permutation(ks[7], n_pool)[:DEC_BATCH * n_pages].reshape(DEC_BATCH, n_pages).astype(jnp.int32)
    col_scale = jnp.ones((N_IN,), jnp.float32).at[2 * QK_W:2 * QK_W + W_A].set(BETA)
    w_in = nrm(ks[8], (DEPTH, D_MODEL, N_IN), D_MODEL ** -0.5) * col_scale
    lambda_q1 = nrm(ks[9], (DEPTH, DK), 0.1)
    lambda_k1 = nrm(ks[10], (DEPTH, DK), 0.1)
    lambda_q2 = nrm(ks[11], (DEPTH, DK), 0.1)
    lambda_k2 = nrm(ks[12], (DEPTH, DK), 0.1)
    attn_norm_g = 1.0 + nrm(ks[13], (DEPTH, DV), 0.02)
    conv_b_w = nrm(ks[14], (DEPTH, CONV_B, W_B), CONV_B ** -0.5)
    conv_b_bias = nrm(ks[15], (DEPTH, W_B), 0.02)
    conv_b_norm_g = 1.0 + nrm(ks[16], (DEPTH, W_B), 0.02)
    conv_b_norm_b = nrm(ks[17], (DEPTH, W_B), 0.02)
    conv_c_w = nrm(ks[18], (DEPTH, CONV_C, W_C), CONV_C ** -0.5)
    w_branch = nrm(ks[19], (DEPTH, N_BRANCH, W_A, D_MODEL), W_A ** -0.5 * BETA)
    w_o = nrm(ks[20], (DEPTH, D_MODEL, D_MODEL), D_MODEL ** -0.5 * BETA)
    ln1_g = 1.0 + nrm(ks[21], (DEPTH, D_MODEL), 0.02)
    ln1_b = nrm(ks[22], (DEPTH, D_MODEL), 0.02)
    w_up = nrm(ks[23], (DEPTH, D_MODEL, 2 * D_FF), D_MODEL ** -0.5 * BETA)
    conv_f_w = nrm(ks[24], (DEPTH, CONV_F, D_FF), CONV_F ** -0.5)
    conv_f_bias = nrm(ks[25], (DEPTH, D_FF), 0.02)
    w_down = nrm(ks[26], (DEPTH, D_FF, D_MODEL), D_FF ** -0.5 * BETA)
    ln2_g = 1.0 + nrm(ks[27], (DEPTH, D_MODEL), 0.02)
    ln2_b = nrm(ks[28], (DEPTH, D_MODEL), 0.02)
    return {"x_prompt": x_prompt, "x_sample": x_sample, "cache_k": cache_k, "cache_v": cache_v,
            "state_conv_b": state_conv_b, "state_conv_c": state_conv_c, "state_conv_ffn": state_conv_ffn,
            "page_table": page_table, "w_in": w_in, "lambda_q1": lambda_q1, "lambda_k1": lambda_k1,
            "lambda_q2": lambda_q2, "lambda_k2": lambda_k2, "attn_norm_g": attn_norm_g, "conv_b_w": conv_b_w,
            "conv_b_bias": conv_b_bias, "conv_b_norm_g": conv_b_norm_g, "conv_b_norm_b": conv_b_norm_b,
            "conv_c_w": conv_c_w, "w_branch": w_branch, "w_o": w_o, "ln1_g": ln1_g, "ln1_b": ln1_b,
            "w_up": w_up, "conv_f_w": conv_f_w, "conv_f_bias": conv_f_bias, "w_down": w_down,
            "ln2_g": ln2_g, "ln2_b": ln2_b}


def reference(x_prompt, x_sample, cache_k, cache_v, state_conv_b, state_conv_c, state_conv_ffn, page_table, w_in,
              lambda_q1, lambda_k1, lambda_q2, lambda_k2, attn_norm_g, conv_b_w, conv_b_bias, conv_b_norm_g,
              conv_b_norm_b, conv_c_w, w_branch, w_o, ln1_g, ln1_b, w_up, conv_f_w, conv_f_bias, w_down, ln2_g, ln2_b):
    slopes = alibi_slopes()
    xp, xs = x_prompt, x_sample
    kp, vp, bp, cp, fp, ksm, vsm, bs, cs, fs = [], [], [], [], [], [], [], [], [], []
    for layer in range(DEPTH):
        lw = (w_in[layer], lambda_q1[layer], lambda_k1[layer], lambda_q2[layer], lambda_k2[layer], attn_norm_g[layer],
              conv_b_w[layer], conv_b_bias[layer], conv_b_norm_g[layer], conv_b_norm_b[layer], conv_c_w[layer],
              w_branch[layer], w_o[layer], ln1_g[layer], ln1_b[layer], w_up[layer], conv_f_w[layer],
              conv_f_bias[layer], w_down[layer], ln2_g[layer], ln2_b[layer])

        def attend_p(q1, q2, k1, k2, v, lam):
            return diff_attn_prompt(q1, q2, k1, k2, v, lam, slopes)

        def attend_s(q1, q2, k1, k2, v, lam, layer_idx=layer):
            return diff_attn_sample(q1, q2, k1, k2, v, lam, cache_k, cache_v, page_table, layer_idx, slopes)

        nbp = xp.shape[0]
        zb = jnp.zeros((nbp, CONV_B - 1, W_B), xp.dtype)
        zc = jnp.zeros((nbp, CONV_C - 1, W_C), xp.dtype)
        zf = jnp.zeros((nbp, CONV_F - 1, D_FF), xp.dtype)
        xp, k_rows, v_rows, nb_, nc_, nf_ = trunk_layer(xp, layer, attend_p, zb, zc, zf, *lw)
        kp.append(k_rows.reshape(-1, PAGE_SIZE, N_HEADS, 2 * DK))
        vp.append(v_rows.reshape(-1, PAGE_SIZE, N_HEADS, DV))
        bp.append(nb_)
        cp.append(nc_)
        fp.append(nf_)
        xs, k_rows, v_rows, nb_, nc_, nf_ = trunk_layer(xs, layer, attend_s, state_conv_b[layer], state_conv_c[layer],
                                                        state_conv_ffn[layer], *lw)
        ksm.append(k_rows)
        vsm.append(v_rows)
        bs.append(nb_)
        cs.append(nc_)
        fs.append(nf_)
    return (xp, xs, jnp.stack(kp), jnp.stack(vp), jnp.stack(bp), jnp.stack(cp), jnp.stack(fp),
            jnp.stack(ksm), jnp.stack(vsm), jnp.stack(bs), jnp.stack(cs), jnp.stack(fs))
```

```python
import functools
import math

import numpy as np
import jax
import jax.numpy as jnp
from jax import lax
from jax.experimental import pallas as pl
from jax.experimental.pallas import tpu as pltpu

F32 = jnp.float32
BF16 = jnp.bfloat16

D_MODEL = 1024
DEPTH = 2
PAST_LEN = 16384
PAGE_SIZE = 128
N_HEADS = 8
DK = 64
DV = 2 * DK
N_BRANCH = 3
CONV_B = 31
CONV_C = 3
CONV_F = 3
D_FF = 2816
EPS = 1e-5
LOG2E = math.log2(math.e)
NEG_BIG = -1e30

VMEM_LIMIT_BYTES = 52 * 1024 * 1024
HALO_B = 32
HALO_C = 8


def _alpha():
    return (2 * DEPTH) ** 0.25


def _lambda_init(layer):
    return 0.8 - 0.6 * math.exp(-0.3 * layer)


def _params(n_axes):
    return pltpu.CompilerParams(dimension_semantics=("arbitrary",) * n_axes,
                                vmem_limit_bytes=VMEM_LIMIT_BYTES)


def _sigmoid(x):
    return 1.0 / (1.0 + jnp.exp(-x))


def _layer_norm(x, g, b):
    mu = jnp.mean(x, axis=-1, keepdims=True)
    xc = x - mu
    var = jnp.mean(xc * xc, axis=-1, keepdims=True)
    return xc * lax.rsqrt(var + EPS) * g + b


def _lam_value(lq1_ref, lk1_ref, lq2_ref, lk2_ref, lam_init):
    a = jnp.sum(lq1_ref[...] * lk1_ref[...], axis=-1, keepdims=True)
    b = jnp.sum(lq2_ref[...] * lk2_ref[...], axis=-1, keepdims=True)
    return jnp.exp(a) - jnp.exp(b) + lam_init


def _slope_log2(h):
    hv = jnp.full((1, 1), h + 1, jnp.int32).astype(F32)
    return jnp.exp2(-hv) * LOG2E


def _proj_body(x_ref, *refs, n_groups, epilogue):
    w_refs = refs[:n_groups]
    out_refs = refs[n_groups:]
    xb = x_ref[...].astype(BF16)
    hs = [jnp.dot(xb, w[...], preferred_element_type=F32) for w in w_refs]
    epilogue(hs, out_refs)


def _proj_call(x, w, layer, col_starts, width, epilogue, out_dtypes, tm, tc, name):
    n, d = x.shape
    n_groups = len(col_starts)
    grid = (width // tc, n // tm)
    in_specs = [pl.BlockSpec((tm, d), lambda c, i: (i, 0))]
    for cs in col_starts:
        in_specs.append(pl.BlockSpec((None, d, tc), lambda c, i, cb=cs // tc: (layer, 0, cb + c)))
    out_specs = [pl.BlockSpec((tm, tc), lambda c, i: (i, c)) for _ in out_dtypes]
    out_shape = [jax.ShapeDtypeStruct((n, width), dt) for dt in out_dtypes]
    return pl.pallas_call(
        functools.partial(_proj_body, n_groups=n_groups, epilogue=epilogue),
        grid=grid, in_specs=in_specs, out_specs=out_specs, out_shape=out_shape,
        compiler_params=_params(2), name=name)(x, *([w] * n_groups))


def _qkv_prompt_epilogue(hs, outs):
    q, k, v = hs
    q_ref, k_ref, v_ref, kb_ref, vb_ref = outs
    q_ref[...] = (q * (DK ** -0.5 * LOG2E)).astype(BF16)
    k_ref[...] = k
    v_ref[...] = v
    kb_ref[...] = k.astype(BF16)
    vb_ref[...] = v.astype(BF16)


def _qkv_sample_epilogue(hs, outs):
    q, k, v = hs
    q_ref, k_ref, v_ref = outs
    q_ref[...] = q * (DK ** -0.5 * LOG2E)
    k_ref[...] = k
    v_ref[...] = v


def _bc_epilogue(hs, outs):
    a_b, g_b, gate_b, gate_c, h_c = hs
    zb_ref, zc_ref, gb_ref = outs
    zb_ref[...] = a_b * _sigmoid(g_b)
    zc_ref[...] = gate_c * h_c
    gb_ref[...] = gate_b


def _gates_epilogue(hs, outs):
    for h, o in zip(hs, outs):
        o[...] = _sigmoid(h)


def _pattn_body(qi_ref, kj_ref, q_ref, k_ref, v_ref, lq1_ref, lk1_ref, lq2_ref, lk2_ref, g_ref,
                o_ref, qm_scr, m_scr, l_scr, acc_scr, rel_scr, bias_scr, *, tq, tk, lam_init):
    h = pl.program_id(1)
    s = pl.program_id(2)
    i = qi_ref[s]
    j = kj_ref[s]
    slope = _slope_log2(h)

    @pl.when(s == 0)
    def _():
        rel = (lax.broadcasted_iota(jnp.int32, (tq, tk), 1)
               - lax.broadcasted_iota(jnp.int32, (tq, tk), 0)).astype(F32)
        rel_scr[...] = rel
        bias_scr[...] = rel * slope

    @pl.when(j == 0)
    def _():
        q = q_ref[...]
        lane = lax.broadcasted_iota(jnp.int32, q.shape, 1)
        zero = jnp.zeros_like(q)
        qm_scr[0] = jnp.where(lane < DK, q, zero)
        qm_scr[1] = jnp.where(lane >= DK, q, zero)
        m_scr[...] = jnp.full(m_scr.shape, NEG_BIG, F32)
        l_scr[...] = jnp.zeros(l_scr.shape, F32)
        acc_scr[...] = jnp.zeros(acc_scr.shape, F32)

    off_i = j * tk - i * tq
    off = jnp.full((1, 1), off_i, jnp.int32).astype(F32)
    c_bias = off * slope

    def step(masked):
        k = k_ref[...]
        v = v_ref[...]
        for mp in range(2):
            sc = lax.dot_general(qm_scr[mp], k, (((1,), (1,)), ((), ())),
                                 preferred_element_type=F32) + bias_scr[...]
            if masked:
                sc = jnp.where(rel_scr[...] + off <= 0.0, sc, NEG_BIG)
            m_prev = m_scr[mp]
            m_new = jnp.maximum(m_prev, jnp.max(sc, axis=-1, keepdims=True) + c_bias)
            alpha = jnp.exp2(m_prev - m_new)
            p = jnp.exp2(sc - (m_new - c_bias))
            l_scr[mp] = alpha * l_scr[mp] + jnp.sum(p, axis=-1, keepdims=True)
            acc_scr[mp] = alpha * acc_scr[mp] + jnp.dot(p.astype(BF16), v, preferred_element_type=F32)
            m_scr[mp] = m_new

    needs_mask = off_i + tk - 1 > 0

    @pl.when(needs_mask)
    def _():
        step(True)

    @pl.when(jnp.logical_not(needs_mask))
    def _():
        step(False)

    @pl.when(j == ((i + 1) * tq) // tk - 1)
    def _():
        lam = _lam_value(lq1_ref, lk1_ref, lq2_ref, lk2_ref, lam_init)
        o = acc_scr[0] / l_scr[0] - lam * (acc_scr[1] / l_scr[1])
        ms = jnp.mean(o * o, axis=-1, keepdims=True)
        o_ref[...] = (o * lax.rsqrt(ms + EPS) * g_ref[...] * (1.0 - lam_init)).astype(o_ref.dtype)


def _prompt_attention(q, kb, vb, lam_p, g, layer, nb, s_len, tq, tk):
    hw = N_HEADS * DV
    q3 = q.reshape(nb, s_len, hw)
    k3 = kb.reshape(nb, s_len, hw)
    v3 = vb.reshape(nb, s_len, hw)
    qi, kj = [], []
    for i in range(s_len // tq):
        for j in range(((i + 1) * tq) // tk):
            qi.append(i)
            kj.append(j)
    n_steps = len(qi)
    qi = jnp.asarray(np.asarray(qi, np.int32))
    kj = jnp.asarray(np.asarray(kj, np.int32))
    vec = lambda w: pl.BlockSpec((None, 1, w), lambda b, h, s, qi_r, kj_r: (layer, 0, 0))
    grid_spec = pltpu.PrefetchScalarGridSpec(
        num_scalar_prefetch=2,
        grid=(nb, N_HEADS, n_steps),
        in_specs=[
            pl.BlockSpec((None, tq, DV), lambda b, h, s, qi_r, kj_r: (b, qi_r[s], h)),
            pl.BlockSpec((None, tk, DV), lambda b, h, s, qi_r, kj_r: (b, kj_r[s], h)),
            pl.BlockSpec((None, tk, DV), lambda b, h, s, qi_r, kj_r: (b, kj_r[s], h)),
            vec(DK), vec(DK), vec(DK), vec(DK), vec(DV),
        ],
        out_specs=pl.BlockSpec((None, tq, DV), lambda b, h, s, qi_r, kj_r: (b, qi_r[s], h)),
        scratch_shapes=[
            pltpu.VMEM((2, tq, DV), BF16),
            pltpu.VMEM((2, tq, 1), F32),
            pltpu.VMEM((2, tq, 1), F32),
            pltpu.VMEM((2, tq, DV), F32),
            pltpu.VMEM((tq, tk), F32),
            pltpu.VMEM((tq, tk), F32),
        ])
    lq1, lk1, lq2, lk2 = lam_p
    o = pl.pallas_call(
        functools.partial(_pattn_body, tq=tq, tk=tk, lam_init=_lambda_init(layer)),
        grid_spec=grid_spec,
        out_shape=jax.ShapeDtypeStruct((nb, s_len, hw), BF16),
        compiler_params=_params(3), name="prompt_attention")(qi, kj, q3, k3, v3, lq1, lk1, lq2, lk2, g)
    return o.reshape(nb * s_len, hw)


def _sattn_body(pt_ref, q_ref, kn_ref, vn_ref, lq1_ref, lk1_ref, lq2_ref, lk2_ref, g_ref, *rest,
                n_pg, t_new, lam_init):
    k_refs = rest[:n_pg]
    v_refs = rest[n_pg:2 * n_pg]
    o_ref = rest[2 * n_pg]
    qm_scr, m_scr, l_scr, acc_scr, pad_scr = rest[2 * n_pg + 1:]
    j = pl.program_id(1)
    nj = pl.num_programs(1)
    nr = 2 * t_new
    nk = n_pg * PAGE_SIZE

    def update(h, sc, v_bf):
        m_prev = m_scr[h]
        m_new = jnp.maximum(m_prev, jnp.max(sc, axis=-1, keepdims=True))
        alpha = jnp.exp2(m_prev - m_new)
        p = jnp.exp2(sc - m_new)
        l_scr[h] = alpha * l_scr[h] + jnp.sum(p, axis=-1, keepdims=True)
        acc_scr[h] = alpha * acc_scr[h] + jnp.dot(p.astype(BF16), v_bf, preferred_element_type=F32)
        m_scr[h] = m_new

    @pl.when(j == 0)
    def _():
        m_scr[...] = jnp.full(m_scr.shape, NEG_BIG, F32)
        l_scr[...] = jnp.zeros(l_scr.shape, F32)
        acc_scr[...] = jnp.zeros(acc_scr.shape, F32)
        pad_scr[...] = jnp.zeros(pad_scr.shape, F32)
        col = lax.broadcasted_iota(jnp.int32, (nr, PAGE_SIZE), 1)
        row = lax.broadcasted_iota(jnp.int32, (nr, PAGE_SIZE), 0)
        tok = jnp.where(row >= t_new, row - t_new, row)
        rel = (col - tok).astype(F32)
        visible = col <= tok
        lane = lax.broadcasted_iota(jnp.int32, (t_new, DV), 1)
        for h in range(N_HEADS):
            q = q_ref[pl.ds(h, t_new, stride=N_HEADS), :]
            zero = jnp.zeros_like(q)
            qm = jnp.concatenate([jnp.where(lane < DK, q, zero), jnp.where(lane >= DK, q, zero)],
                                 axis=0).astype(BF16)
            qm_scr[h] = qm
            pad_scr[0, 0:t_new, :] = kn_ref[pl.ds(h, t_new, stride=N_HEADS), :]
            pad_scr[1, 0:t_new, :] = vn_ref[pl.ds(h, t_new, stride=N_HEADS), :]
            sc = lax.dot_general(qm, pad_scr[0].astype(BF16), (((1,), (1,)), ((), ())),
                                 preferred_element_type=F32) + rel * _slope_log2(h)
            sc = jnp.where(visible, sc, NEG_BIG)
            update(h, sc, pad_scr[1].astype(BF16))

    col = lax.broadcasted_iota(jnp.int32, (nr, nk), 1)
    row = lax.broadcasted_iota(jnp.int32, (nr, nk), 0)
    tok = jnp.where(row >= t_new, row - t_new, row)
    base = jnp.full((1, 1), j * nk - PAST_LEN, jnp.int32).astype(F32)
    rel = (col - tok).astype(F32) + base
    for h in range(N_HEADS):
        k = jnp.concatenate([r[pl.ds(h, PAGE_SIZE, stride=N_HEADS), :] for r in k_refs], axis=0)
        v = jnp.concatenate([r[pl.ds(h, PAGE_SIZE, stride=N_HEADS), :] for r in v_refs], axis=0)
        sc = lax.dot_general(qm_scr[h], k.astype(BF16), (((1,), (1,)), ((), ())),
                             preferred_element_type=F32) + rel * _slope_log2(h)
        update(h, sc, v.astype(BF16))

    @pl.when(j == nj - 1)
    def _():
        lam = _lam_value(lq1_ref, lk1_ref, lq2_ref, lk2_ref, lam_init)
        for h in range(N_HEADS):
            acc = acc_scr[h]
            l = l_scr[h]
            o = acc[0:t_new] / l[0:t_new] - lam * (acc[t_new:nr] / l[t_new:nr])
            ms = jnp.mean(o * o, axis=-1, keepdims=True)
            o_ref[:, h * DV:(h + 1) * DV] = o * lax.rsqrt(ms + EPS) * g_ref[...] * (1.0 - lam_init)


def _sample_attention(q, k_new, v_new, cache_k, cache_v, page_table, lam_p, g, layer, n_pg):
    nd, rows, _ = q.shape
    t_new = rows // N_HEADS
    n_pages = page_table.shape[1]
    pt = page_table.reshape(-1)
    page_rows = PAGE_SIZE * N_HEADS

    def page_spec(gi):
        return pl.BlockSpec((None, None, page_rows, DV),
                            lambda b, j, pt_r: (layer, pt_r[b * n_pages + j * n_pg + gi], 0, 0))

    seq = pl.BlockSpec((None, rows, DV), lambda b, j, pt_r: (b, 0, 0))
    vec = lambda w: pl.BlockSpec((None, 1, w), lambda b, j, pt_r: (layer, 0, 0))
    grid_spec = pltpu.PrefetchScalarGridSpec(
        num_scalar_prefetch=1,
        grid=(nd, n_pages // n_pg),
        in_specs=[seq, seq, seq, vec(DK), vec(DK), vec(DK), vec(DK), vec(DV)]
                 + [page_spec(gi) for gi in range(n_pg)] * 2,
        out_specs=pl.BlockSpec((t_new, N_HEADS * DV), lambda b, j, pt_r: (b, 0)),
        scratch_shapes=[
            pltpu.VMEM((N_HEADS, 2 * t_new, DV), BF16),
            pltpu.VMEM((N_HEADS, 2 * t_new, 1), F32),
            pltpu.VMEM((N_HEADS, 2 * t_new, 1), F32),
            pltpu.VMEM((N_HEADS, 2 * t_new, DV), F32),
            pltpu.VMEM((2, PAGE_SIZE, DV), F32),
        ])
    lq1, lk1, lq2, lk2 = lam_p
    return pl.pallas_call(
        functools.partial(_sattn_body, n_pg=n_pg, t_new=t_new, lam_init=_lambda_init(layer)),
        grid_spec=grid_spec,
        out_shape=jax.ShapeDtypeStruct((nd * t_new, N_HEADS * DV), F32),
        compiler_params=_params(2), name="sample_attention")(
            pt, q, k_new, v_new, lq1, lk1, lq2, lk2, g, *([cache_k] * n_pg), *([cache_v] * n_pg))


def _conv_body(zb_ref, zbh_ref, zc_ref, zch_ref, gb_ref, wb_ref, bb_ref, ng_ref, nb_ref, wc_ref,
               ob_ref, oc_ref, winb_scr, winc_scr, cb_scr, *, tt, tiles_per_seq, zero_start, rc):
    i = pl.program_id(0)
    n_slab = zb_ref.shape[1] // 128
    if zero_start:
        keep = jnp.where(i % tiles_per_seq == 0, 0.0, 1.0)
    else:
        keep = 1.0
    for c in range(n_slab):
        lanes = slice(c * 128, (c + 1) * 128)
        winb_scr[c, 0:HALO_B, :] = zbh_ref[:, lanes] * keep
        winb_scr[c, HALO_B:HALO_B + tt, :] = zb_ref[:, lanes]
        winc_scr[c, 0:HALO_C, :] = zch_ref[:, lanes] * keep
        winc_scr[c, HALO_C:HALO_C + tt, :] = zc_ref[:, lanes]

    for c in range(n_slab):
        lanes = slice(c * 128, (c + 1) * 128)

        def chunk(r, carry):
            r0 = pl.multiple_of(r * rc, rc)
            acc = jnp.zeros((rc, 128), F32)
            for k in range(CONV_B):
                acc = acc + wb_ref[k:k + 1, lanes] * winb_scr[c, pl.ds(r0 + HALO_B - (CONV_B - 1) + k, rc), :]
            cb_scr[pl.ds(r0, rc), lanes] = acc
            acc_c = jnp.zeros((rc, 128), F32)
            for k in range(CONV_C):
                acc_c = acc_c + wc_ref[k:k + 1, lanes] * winc_scr[c, pl.ds(r0 + HALO_C - (CONV_C - 1) + k, rc), :]
            oc_ref[pl.ds(r0, rc), lanes] = (gb_ref[pl.ds(r0, rc), lanes] * acc_c).astype(oc_ref.dtype)
            return carry

        lax.fori_loop(0, tt // rc, chunk, 0)

    y = _layer_norm(cb_scr[...] + bb_ref[...], ng_ref[...], nb_ref[...])
    ob_ref[...] = (y * _sigmoid(y)).astype(ob_ref.dtype)


def _conv_call(zb, zb_halo_src, zc, zc_halo_src, gate_b, wb, bb, ng, nb_, wc, layer, tt, seq_len,
               zero_start, out_dtype):
    n, c = zb.shape
    tiles_per_seq = max(seq_len // tt, 1)
    rc = min(tt, 32)
    if zero_start:
        hb_map = lambda i: (jnp.maximum(i * (tt // HALO_B) - 1, 0), 0)
        hc_map = lambda i: (jnp.maximum(i * (tt // HALO_C) - 1, 0), 0)
    else:
        hb_map = lambda i: (i, 0)
        hc_map = lambda i: (i, 0)
    main = pl.BlockSpec((tt, c), lambda i: (i, 0))
    vec = pl.BlockSpec((None, 1, c), lambda i: (layer, 0, 0))
    return pl.pallas_call(
        functools.partial(_conv_body, tt=tt, tiles_per_seq=tiles_per_seq, zero_start=zero_start, rc=rc),
        grid=(n // tt,),
        in_specs=[main, pl.BlockSpec((HALO_B, c), hb_map), main, pl.BlockSpec((HALO_C, c), hc_map), main,
                  pl.BlockSpec((None, HALO_B, c), lambda i: (layer, 0, 0)), vec, vec, vec,
                  pl.BlockSpec((None, 8, c), lambda i: (layer, 0, 0))],
        out_specs=[main, main],
        out_shape=[jax.ShapeDtypeStruct((n, c), out_dtype)] * 2,
        scratch_shapes=[pltpu.VMEM((c // 128, HALO_B + tt, 128), F32),
                        pltpu.VMEM((c // 128, HALO_C + tt, 128), F32),
                        pltpu.VMEM((tt, c), F32)],
        compiler_params=_params(1), name="branch_convs")(
            zb, zb_halo_src, zc, zc_halo_src, gate_b, wb, bb, ng, nb_, wc)


def _merge_body(x_ref, oa_ref, ob_ref, oc_ref, g0_ref, g1_ref, g2_ref, wbr_ref, wo_ref, lg_ref, lb_ref,
                y_ref, *, alpha):
    merged = g0_ref[...] * jnp.dot(oa_ref[...].astype(BF16), wbr_ref[0], preferred_element_type=F32)
    merged = merged + g1_ref[...] * jnp.dot(ob_ref[...].astype(BF16), wbr_ref[1], preferred_element_type=F32)
    merged = merged + g2_ref[...] * jnp.dot(oc_ref[...].astype(BF16), wbr_ref[2], preferred_element_type=F32)
    y = jnp.dot(merged.astype(BF16), wo_ref[...], preferred_element_type=F32)
    y_ref[...] = _layer_norm(alpha * x_ref[...] + y, lg_ref[...], lb_ref[...])


def _merge_call(x, oa, ob, oc, gates, w_br, w_o, ln_g, ln_b, layer, tm):
    n, d = x.shape
    row = pl.BlockSpec((tm, d), lambda i: (i, 0))
    vec = pl.BlockSpec((None, 1, d), lambda i: (layer, 0, 0))
    once = pl.Buffered(1)
    return pl.pallas_call(
        functools.partial(_merge_body, alpha=_alpha()),
        grid=(n // tm,),
        in_specs=[row] * 7 + [
            pl.BlockSpec((None, N_BRANCH, d, d), lambda i: (layer, 0, 0, 0), pipeline_mode=once),
            pl.BlockSpec((None, d, d), lambda i: (layer, 0, 0), pipeline_mode=once), vec, vec],
        out_specs=row,
        out_shape=jax.ShapeDtypeStruct((n, d), F32),
        compiler_params=_params(1), name="merge")(x, oa, ob, oc, *gates, w_br, w_o, ln_g, ln_b)


FF_CHUNK = 256


def _ffn_body(x_ref, *rest, alpha, tm, seq_len, tiles_per_seq):
    carried = seq_len >= tm
    if carried:
        wup_ref, wdn_ref, cw_ref, cbias_ref, lg_ref, lb_ref, y_ref, tail_ref, gbuf_scr, carry_scr = rest
    else:
        (s1_ref, s2_ref, wup_ref, wdn_ref, cw_ref, cbias_ref, lg_ref, lb_ref, y_ref, tail_ref,
         gbuf_scr) = rest
    i = pl.program_id(0)
    x = x_ref[...]
    xb = x.astype(BF16)
    acc = jnp.zeros((tm, x.shape[1]), F32)

    if carried:
        @pl.when(i % tiles_per_seq == 0)
        def _():
            carry_scr[...] = jnp.zeros(carry_scr.shape, F32)
    else:
        pos = lax.broadcasted_iota(jnp.int32, (tm, FF_CHUNK), 0) % seq_len
        gbuf_scr[0:8, :] = jnp.zeros((8, FF_CHUNK), F32)

    for c in range(D_FF // FF_CHUNK):
        cols = slice(c * FF_CHUNK, (c + 1) * FF_CHUNK)
        gcols = slice(D_FF + c * FF_CHUNK, D_FF + (c + 1) * FF_CHUNK)
        u = jnp.dot(xb, wup_ref[:, cols], preferred_element_type=F32)
        g = jnp.dot(xb, wup_ref[:, gcols], preferred_element_type=F32)
        if carried:
            gbuf_scr[0:8, :] = carry_scr[:, cols]
        gbuf_scr[8:8 + tm, :] = g
        g1 = gbuf_scr[pl.ds(7, tm), :]
        g2 = gbuf_scr[pl.ds(6, tm), :]
        if carried:
            carry_scr[:, cols] = g[tm - 8:tm, :]
            tail_ref[:, cols] = g[tm - 8:tm, :]
        else:
            g1 = jnp.where(pos >= 1, g1, 0.0) + s1_ref[:, cols]
            g2 = jnp.where(pos >= 2, g2, 0.0) + s2_ref[:, cols]
            tail_ref[:, cols] = g
        cf = cw_ref[0:1, cols] * g2 + cw_ref[1:2, cols] * g1 + cw_ref[2:3, cols] * g + cbias_ref[:, cols]
        act = cf * _sigmoid(cf) * u
        acc = acc + jnp.dot(act.astype(BF16), wdn_ref[cols, :], preferred_element_type=F32)

    y_ref[...] = _layer_norm(alpha * x + acc, lg_ref[...], lb_ref[...])


def _ffn_call(x, w_up, w_dn, cw, cbias, ln_g, ln_b, layer, tm, seq_len, inject=None):
    n, d = x.shape
    carried = seq_len >= tm
    tiles_per_seq = max(seq_len // tm, 1)
    row = pl.BlockSpec((tm, d), lambda i: (i, 0))
    vec = pl.BlockSpec((None, 1, d), lambda i: (layer, 0, 0))
    once = pl.Buffered(1)
    in_specs = [row]
    args = [x]
    if not carried:
        wide = pl.BlockSpec((tm, D_FF), lambda i: (i, 0))
        in_specs += [wide, wide]
        args += list(inject)
    in_specs += [pl.BlockSpec((None, d, 2 * D_FF), lambda i: (layer, 0, 0), pipeline_mode=once),
                 pl.BlockSpec((None, D_FF, d), lambda i: (layer, 0, 0), pipeline_mode=once),
                 pl.BlockSpec((None, 8, D_FF), lambda i: (layer, 0, 0)),
                 pl.BlockSpec((None, 1, D_FF), lambda i: (layer, 0, 0)), vec, vec]
    args += [w_up, w_dn, cw, cbias, ln_g, ln_b]
    scratch = [pltpu.VMEM((8 + tm, FF_CHUNK), F32)]
    if carried:
        scratch.append(pltpu.VMEM((8, D_FF), F32))
        tail_shape = jax.ShapeDtypeStruct((n // tm, 8, D_FF), F32)
        tail_spec = pl.BlockSpec((None, 8, D_FF), lambda i: (i, 0, 0))
    else:
        tail_shape = jax.ShapeDtypeStruct((n, D_FF), F32)
        tail_spec = pl.BlockSpec((tm, D_FF), lambda i: (i, 0))
    return pl.pallas_call(
        functools.partial(_ffn_body, alpha=_alpha(), tm=tm, seq_len=seq_len, tiles_per_seq=tiles_per_seq),
        grid=(n // tm,), in_specs=in_specs, out_specs=[row, tail_spec],
        out_shape=[jax.ShapeDtypeStruct((n, d), F32), tail_shape],
        scratch_shapes=scratch, compiler_params=_params(1), name="conv_ffn")(*args)


def _pad_rows_front(a, rows):
    return jnp.pad(a, ((0, 0), (0, 0), (rows - a.shape[2], 0), (0, 0)))


def _layer(x, layer, wts, nseq, seq_len, prompt, cfg, sample_in=None):
    (w_in, lam_p, attn_g, cb_w, cb_bias, cb_ng, cb_nb, cc_w, w_br, w_o, ln1_g, ln1_b, w_up, cf_w, cf_bias,
     w_dn, ln2_g, ln2_b) = wts
    d = D_MODEL
    qk_w = N_HEADS * 2 * DK
    o_b0 = 2 * qk_w + N_HEADS * DV
    o_c0 = o_b0 + 2 * d
    o_g0 = o_c0 + 3 * d
    tm, tc = cfg["tm"], cfg["tc"]

    zb, zc, gate_b = _proj_call(x, w_in, layer, [o_b0, o_b0 + d, o_c0, o_c0 + d, o_c0 + 2 * d], d,
                                _bc_epilogue, [F32] * 3, tm, tc, "proj_bc")
    gates = _proj_call(x, w_in, layer, [o_g0, o_g0 + d, o_g0 + 2 * d], d, _gates_epilogue, [F32] * 3,
                       tm, tc, "proj_gates")

    if prompt:
        q, k, v, kb, vb = _proj_call(x, w_in, layer, [0, qk_w, 2 * qk_w], d, _qkv_prompt_epilogue,
                                     [BF16, F32, F32, BF16, BF16], tm, tc, "proj_qkv")
        o_a = _prompt_attention(q, kb, vb, lam_p, attn_g, layer, nseq, seq_len, cfg["tq"], cfg["tk"])
        o_b, o_c = _conv_call(zb, zb, zc, zc, gate_b, cb_w, cb_bias, cb_ng, cb_nb, cc_w, layer,
                              cfg["tt"], seq_len, True, BF16)
    else:
        cache_k, cache_v, page_table, st_b, st_c, st_f = sample_in
        q, k, v = _proj_call(x, w_in, layer, [0, qk_w, 2 * qk_w], d, _qkv_sample_epilogue, [F32] * 3,
                             tm, tc, "proj_qkv")
        rows = seq_len * N_HEADS
        to_rows = lambda a: a.reshape(nseq, rows, DV)
        o_a = _sample_attention(to_rows(q), to_rows(k), to_rows(v), cache_k, cache_v, page_table, lam_p,
                                attn_g, layer, cfg["n_pg"])
        o_b, o_c = _conv_call(zb, st_b[layer].reshape(nseq * HALO_B, d), zc,
                              st_c[layer].reshape(nseq * HALO_C, d), gate_b, cb_w, cb_bias, cb_ng, cb_nb,
                              cc_w, layer, seq_len, seq_len, False, F32)

    x1 = _merge_call(x, o_a, o_b, o_c, gates, w_br, w_o, ln1_g, ln1_b, layer, cfg["tm_merge"])

    if prompt:
        x2, tail = _ffn_call(x1, w_up, w_dn, cf_w, cf_bias, ln2_g, ln2_b, layer, cfg["tm_ffn"], seq_len)
        tps = seq_len // cfg["tm_ffn"]
        new_f = tail.reshape(nseq, tps, 8, D_FF)[:, tps - 1, 8 - (CONV_F - 1):, :]
        new_b = zb.reshape(nseq, seq_len, d)[:, seq_len - (CONV_B - 1):, :]
        new_c = zc.reshape(nseq, seq_len, d)[:, seq_len - (CONV_C - 1):, :]
    else:
        sf = st_f[layer]
        zeros = jnp.zeros((nseq, seq_len, D_FF), F32)
        s1 = zeros.at[:, 0].set(sf[:, 1]).reshape(nseq * seq_len, D_FF)
        s2 = zeros.at[:, 0].set(sf[:, 0]).at[:, 1].set(sf[:, 1]).reshape(nseq * seq_len, D_FF)
        x2, g_f = _ffn_call(x1, w_up, w_dn, cf_w, cf_bias, ln2_g, ln2_b, layer, nseq * seq_len, seq_len,
                            inject=(s1, s2))
        new_f = g_f.reshape(nseq, seq_len, D_FF)[:, seq_len - (CONV_F - 1):, :]
        new_b = jnp.concatenate([st_b[layer][:, HALO_B - (CONV_B - 1) + seq_len:, :],
                                 zb.reshape(nseq, seq_len, d)], axis=1)
        new_c = zc.reshape(nseq, seq_len, d)[:, seq_len - (CONV_C - 1):, :]
    return x2, k, v, new_b, new_c, new_f


PROMPT_CFG = dict(tm=512, tc=512, tq=1024, tk=512, tt=256, tm_merge=256, tm_ffn=256)
SAMPLE_CFG = dict(tm=256, tc=512, n_pg=4, tm_merge=256)


def kernel(x_prompt, x_sample, cache_k, cache_v, state_conv_b, state_conv_c, state_conv_ffn, page_table, w_in,
           lambda_q1, lambda_k1, lambda_q2, lambda_k2, attn_norm_g, conv_b_w, conv_b_bias, conv_b_norm_g,
           conv_b_norm_b, conv_c_w, w_branch, w_o, ln1_g, ln1_b, w_up, conv_f_w, conv_f_bias, w_down, ln2_g,
           ln2_b):
    nbp, s_len, d = x_prompt.shape
    nd, t_new, _ = x_sample.shape
    depth = w_in.shape[0]
    row3 = lambda a: a.reshape(depth, 1, a.shape[-1])
    pad8 = lambda a: jnp.pad(a, ((0, 0), (0, 8 - a.shape[1] % 8 if a.shape[1] % 8 else 0), (0, 0)))
    wts = (w_in.astype(BF16),
           (row3(lambda_q1), row3(lambda_k1), row3(lambda_q2), row3(lambda_k2)),
           row3(attn_norm_g), pad8(conv_b_w), row3(conv_b_bias), row3(conv_b_norm_g), row3(conv_b_norm_b),
           pad8(conv_c_w), w_branch.astype(BF16), w_o.astype(BF16), row3(ln1_g), row3(ln1_b),
           w_up.astype(BF16), pad8(conv_f_w), row3(conv_f_bias), w_down.astype(BF16), row3(ln2_g), row3(ln2_b))
    pool = cache_k.shape[1]
    ck = cache_k.reshape(depth, pool, PAGE_SIZE * N_HEADS, 2 * DK)
    cv = cache_v.reshape(depth, pool, PAGE_SIZE * N_HEADS, DV)
    sample_in = (ck, cv, page_table, _pad_rows_front(state_conv_b, HALO_B),
                 _pad_rows_front(state_conv_c, HALO_C), state_conv_ffn)

    xp = x_prompt.reshape(nbp * s_len, d)
    xs = x_sample.reshape(nd * t_new, d)
    outs = [[] for _ in range(10)]
    for layer in range(depth):
        xp, k, v, nb_, nc_, nf_ = _layer(xp, layer, wts, nbp, s_len, True, PROMPT_CFG)
        outs[0].append(k.reshape(-1, PAGE_SIZE, N_HEADS, 2 * DK))
        outs[1].append(v.reshape(-1, PAGE_SIZE, N_HEADS, DV))
        outs[2].append(nb_)
        outs[3].append(nc_)
        outs[4].append(nf_)
        xs, k, v, nb_, nc_, nf_ = _layer(xs, layer, wts, nd, t_new, False, SAMPLE_CFG, sample_in)
        outs[5].append(k.reshape(nd, t_new, N_HEADS, 2 * DK))
        outs[6].append(v.reshape(nd, t_new, N_HEADS, DV))
        outs[7].append(nb_)
        outs[8].append(nc_)
        outs[9].append(nf_)
    return (xp.reshape(nbp, s_len, d), xs.reshape(nd, t_new, d)) + tuple(jnp.stack(o) for o in outs)
```

```python
import functools
import math

import numpy as np
import jax
import jax.numpy as jnp
from jax import lax
from jax.experimental import pallas as pl
from jax.experimental.pallas import tpu as pltpu

F32 = jnp.float32
BF16 = jnp.bfloat16

D_MODEL = 1024
DEPTH = 2
PAST_LEN = 16384
PAGE_SIZE = 128
N_HEADS = 8
DK = 64
DV = 2 * DK
N_BRANCH = 3
CONV_B = 31
CONV_C = 3
CONV_F = 3
D_FF = 2816
EPS = 1e-5
LOG2E = math.log2(math.e)
NEG_BIG = -1e30

VMEM_LIMIT_BYTES = 52 * 1024 * 1024
HALO_B = 32
HALO_C = 8


def _alpha():
    return (2 * DEPTH) ** 0.25


def _lambda_init(layer):
    return 0.8 - 0.6 * math.exp(-0.3 * layer)


def _params(n_axes):
    return pltpu.CompilerParams(dimension_semantics=("arbitrary",) * n_axes,
                                vmem_limit_bytes=VMEM_LIMIT_BYTES)


def _sigmoid(x):
    return 1.0 / (1.0 + jnp.exp(-x))


def _layer_norm(x, g, b):
    mu = jnp.mean(x, axis=-1, keepdims=True)
    xc = x - mu
    var = jnp.mean(xc * xc, axis=-1, keepdims=True)
    return xc * lax.rsqrt(var + EPS) * g + b


def _lam_value(lq1_ref, lk1_ref, lq2_ref, lk2_ref, lam_init):
    a = jnp.sum(lq1_ref[...] * lk1_ref[...], axis=-1, keepdims=True)
    b = jnp.sum(lq2_ref[...] * lk2_ref[...], axis=-1, keepdims=True)
    return jnp.exp(a) - jnp.exp(b) + lam_init


def _slope_log2(h):
    hv = jnp.full((1, 1), h + 1, jnp.int32).astype(F32)
    return jnp.exp2(-hv) * LOG2E


def _proj_body(x_ref, *refs, n_groups, epilogue):
    w_refs = refs[:n_groups]
    out_refs = refs[n_groups:]
    xb = x_ref[...].astype(BF16)
    hs = [jnp.dot(xb, w[...], preferred_element_type=F32) for w in w_refs]
    epilogue(hs, out_refs)


def _proj_call(x, w, layer, col_starts, width, epilogue, out_dtypes, tm, tc, name):
    n, d = x.shape
    n_groups = len(col_starts)
    grid = (width // tc, n // tm)
    in_specs = [pl.BlockSpec((tm, d), lambda c, i: (i, 0))]
    for cs in col_starts:
        in_specs.append(pl.BlockSpec((None, d, tc), lambda c, i, cb=cs // tc: (layer, 0, cb + c)))
    out_specs = [pl.BlockSpec((tm, tc), lambda c, i: (i, c)) for _ in out_dtypes]
    out_shape = [jax.ShapeDtypeStruct((n, width), dt) for dt in out_dtypes]
    return pl.pallas_call(
        functools.partial(_proj_body, n_groups=n_groups, epilogue=epilogue),
        grid=grid, in_specs=in_specs, out_specs=out_specs, out_shape=out_shape,
        compiler_params=_params(2), name=name)(x, *([w] * n_groups))


def _qkv_prompt_epilogue(hs, outs):
    q, k, v = hs
    q_ref, k_ref, v_ref, kb_ref, vb_ref = outs
    q_ref[...] = (q * (DK ** -0.5 * LOG2E)).astype(BF16)
    k_ref[...] = k
    v_ref[...] = v
    kb_ref[...] = k.astype(BF16)
    vb_ref[...] = v.astype(BF16)


def _qkv_sample_epilogue(hs, outs):
    q, k, v = hs
    q_ref, k_ref, v_ref = outs
    q_ref[...] = q * (DK ** -0.5 * LOG2E)
    k_ref[...] = k
    v_ref[...] = v


def _bc_epilogue(hs, outs):
    a_b, g_b, gate_b, gate_c, h_c = hs
    zb_ref, zc_ref, gb_ref = outs
    zb_ref[...] = a_b * _sigmoid(g_b)
    zc_ref[...] = gate_c * h_c
    gb_ref[...] = gate_b


def _gates_epilogue(hs, outs):
    for h, o in zip(hs, outs):
        o[...] = _sigmoid(h)


def _pattn_body(qi_ref, kj_ref, q_ref, k_ref, vt_ref, lq1_ref, lk1_ref, lq2_ref, lk2_ref, g_ref,
                o_ref, qm_scr, m_scr, l_scr, acc_scr, rel_scr, bias_scr, *, tq, tk, lam_init):
    h = pl.program_id(1)
    s = pl.program_id(2)
    i = qi_ref[s]
    j = kj_ref[s]
    slope = _slope_log2(h)

    @pl.when(s == 0)
    def _():
        rel = (lax.broadcasted_iota(jnp.int32, (tk, tq), 0)
               - lax.broadcasted_iota(jnp.int32, (tk, tq), 1)).astype(F32)
        rel_scr[...] = rel
        bias_scr[...] = rel * slope

    @pl.when(j == 0)
    def _():
        q = q_ref[...]
        lane = lax.broadcasted_iota(jnp.int32, q.shape, 1)
        zero = jnp.zeros_like(q)
        qm_scr[0] = jnp.where(lane < DK, q, zero)
        qm_scr[1] = jnp.where(lane >= DK, q, zero)
        m_scr[...] = jnp.full(m_scr.shape, NEG_BIG, F32)
        l_scr[...] = jnp.zeros(l_scr.shape, F32)
        acc_scr[...] = jnp.zeros(acc_scr.shape, F32)

    off_i = j * tk - i * tq
    off = jnp.full((1, 1), off_i, jnp.int32).astype(F32)
    c_bias = off * slope

    def step(masked):
        k = k_ref[...]
        vt = vt_ref[...]
        for mp in range(2):
            sc = lax.dot_general(k, qm_scr[mp], (((1,), (1,)), ((), ())),
                                 preferred_element_type=F32) + bias_scr[...]
            if masked:
                sc = jnp.where(rel_scr[...] + off <= 0.0, sc, NEG_BIG)
            m_prev = m_scr[mp]
            m_new = jnp.maximum(m_prev, jnp.max(sc, axis=0, keepdims=True) + c_bias)
            alpha = jnp.exp2(m_prev - m_new)
            p = jnp.exp2(sc - (m_new - c_bias))
            l_scr[mp] = alpha * l_scr[mp] + jnp.sum(p, axis=0, keepdims=True)
            acc_scr[mp] = alpha * acc_scr[mp] + jnp.dot(vt, p.astype(BF16), preferred_element_type=F32)
            m_scr[mp] = m_new

    needs_mask = off_i + tk - 1 > 0

    @pl.when(needs_mask)
    def _():
        step(True)

    @pl.when(jnp.logical_not(needs_mask))
    def _():
        step(False)

    @pl.when(j == ((i + 1) * tq) // tk - 1)
    def _():
        lam = _lam_value(lq1_ref, lk1_ref, lq2_ref, lk2_ref, lam_init)
        ot = acc_scr[0] / l_scr[0] - lam * (acc_scr[1] / l_scr[1])
        ms = jnp.mean(ot * ot, axis=0, keepdims=True)
        o = (ot * lax.rsqrt(ms + EPS)).T
        o_ref[...] = (o * g_ref[...] * (1.0 - lam_init)).astype(o_ref.dtype)


def _prompt_attention(q, kb, vb, lam_p, g, layer, nb, s_len, tq, tk):
    hw = N_HEADS * DV
    q3 = q.reshape(nb, s_len, hw)
    k3 = kb.reshape(nb, s_len, hw)
    vt = jnp.transpose(vb.reshape(nb, s_len, N_HEADS, DV), (0, 2, 3, 1))
    qi, kj = [], []
    for i in range(s_len // tq):
        for j in range(((i + 1) * tq) // tk):
            qi.append(i)
            kj.append(j)
    n_steps = len(qi)
    qi = jnp.asarray(np.asarray(qi, np.int32))
    kj = jnp.asarray(np.asarray(kj, np.int32))
    vec = lambda w: pl.BlockSpec((None, 1, w), lambda b, h, s, qi_r, kj_r: (layer, 0, 0))
    grid_spec = pltpu.PrefetchScalarGridSpec(
        num_scalar_prefetch=2,
        grid=(nb, N_HEADS, n_steps),
        in_specs=[
            pl.BlockSpec((None, tq, DV), lambda b, h, s, qi_r, kj_r: (b, qi_r[s], h)),
            pl.BlockSpec((None, tk, DV), lambda b, h, s, qi_r, kj_r: (b, kj_r[s], h)),
            pl.BlockSpec((None, None, DV, tk), lambda b, h, s, qi_r, kj_r: (b, h, 0, kj_r[s])),
            vec(DK), vec(DK), vec(DK), vec(DK), vec(DV),
        ],
        out_specs=pl.BlockSpec((None, tq, DV), lambda b, h, s, qi_r, kj_r: (b, qi_r[s], h)),
        scratch_shapes=[
            pltpu.VMEM((2, tq, DV), BF16),
            pltpu.VMEM((2, 1, tq), F32),
            pltpu.VMEM((2, 1, tq), F32),
            pltpu.VMEM((2, DV, tq), F32),
            pltpu.VMEM((tk, tq), F32),
            pltpu.VMEM((tk, tq), F32),
        ])
    lq1, lk1, lq2, lk2 = lam_p
    o = pl.pallas_call(
        functools.partial(_pattn_body, tq=tq, tk=tk, lam_init=_lambda_init(layer)),
        grid_spec=grid_spec,
        out_shape=jax.ShapeDtypeStruct((nb, s_len, hw), BF16),
        compiler_params=_params(3), name="prompt_attention")(qi, kj, q3, k3, vt, lq1, lk1, lq2, lk2, g)
    return o.reshape(nb * s_len, hw)


def _sattn_body(pt_ref, q_ref, kn_ref, vn_ref, lq1_ref, lk1_ref, lq2_ref, lk2_ref, g_ref, *rest,
                n_pg, t_new, lam_init):
    k_refs = rest[:n_pg]
    v_refs = rest[n_pg:2 * n_pg]
    o_ref = rest[2 * n_pg]
    qm_scr, m_scr, l_scr, acc_scr, pad_scr = rest[2 * n_pg + 1:]
    j = pl.program_id(1)
    nj = pl.num_programs(1)
    nr = 2 * t_new
    nrows = N_HEADS * nr
    nk = n_pg * PAGE_SIZE

    ridx = lax.broadcasted_iota(jnp.int32, (nrows, 1), 0)
    head1 = (lax.shift_right_logical(ridx, int(math.log2(nr))) + 1).astype(F32)
    slope = jnp.exp2(-head1) * LOG2E
    tok = jnp.bitwise_and(ridx, t_new - 1)

    def scores(k_heads):
        return jnp.concatenate(
            [lax.dot_general(qm_scr[h], k_heads[h], (((1,), (1,)), ((), ())), preferred_element_type=F32)
             for h in range(N_HEADS)], axis=0)

    def update(sc, v_heads):
        m_prev = m_scr[...]
        m_new = jnp.maximum(m_prev, jnp.max(sc, axis=-1, keepdims=True))
        alpha = jnp.exp2(m_prev - m_new)
        p = jnp.exp2(sc - m_new)
        l_scr[...] = alpha * l_scr[...] + jnp.sum(p, axis=-1, keepdims=True)
        pb = p.astype(BF16)
        pv = jnp.concatenate(
            [jnp.dot(pb[h * nr:(h + 1) * nr], v_heads[h], preferred_element_type=F32)
             for h in range(N_HEADS)], axis=0)
        acc_scr[...] = alpha * acc_scr[...] + pv
        m_scr[...] = m_new

    @pl.when(j == 0)
    def _():
        m_scr[...] = jnp.full(m_scr.shape, NEG_BIG, F32)
        l_scr[...] = jnp.zeros(l_scr.shape, F32)
        acc_scr[...] = jnp.zeros(acc_scr.shape, F32)
        pad_scr[...] = jnp.zeros(pad_scr.shape, F32)
        lane = lax.broadcasted_iota(jnp.int32, (t_new, DV), 1)
        for h in range(N_HEADS):
            q = q_ref[pl.ds(h, t_new, stride=N_HEADS), :]
            zero = jnp.zeros_like(q)
            qm_scr[h] = jnp.concatenate([jnp.where(lane < DK, q, zero), jnp.where(lane >= DK, q, zero)],
                                        axis=0).astype(BF16)
            pad_scr[0, h, 0:t_new, :] = kn_ref[pl.ds(h, t_new, stride=N_HEADS), :]
            pad_scr[1, h, 0:t_new, :] = vn_ref[pl.ds(h, t_new, stride=N_HEADS), :]
        col = lax.broadcasted_iota(jnp.int32, (1, PAGE_SIZE), 1)
        sc = scores([pad_scr[0, h].astype(BF16) for h in range(N_HEADS)])
        sc = sc + (col - tok).astype(F32) * slope
        sc = jnp.where(col <= tok, sc, NEG_BIG)
        update(sc, [pad_scr[1, h].astype(BF16) for h in range(N_HEADS)])

    def head_rows(refs, h):
        return jnp.concatenate([r[pl.ds(h, PAGE_SIZE, stride=N_HEADS), :] for r in refs], axis=0).astype(BF16)

    colf = lax.broadcasted_iota(jnp.int32, (1, nk), 1).astype(F32)
    base = (jnp.full((1, 1), j * nk - PAST_LEN, jnp.int32) - tok).astype(F32)
    sc = scores([head_rows(k_refs, h) for h in range(N_HEADS)])
    sc = sc + (colf + base) * slope
    update(sc, [head_rows(v_refs, h) for h in range(N_HEADS)])

    @pl.when(j == nj - 1)
    def _():
        lam = _lam_value(lq1_ref, lk1_ref, lq2_ref, lk2_ref, lam_init)
        o_all = acc_scr[...] / l_scr[...]
        for h in range(N_HEADS):
            r0 = h * nr
            o = o_all[r0:r0 + t_new] - lam * o_all[r0 + t_new:r0 + nr]
            ms = jnp.mean(o * o, axis=-1, keepdims=True)
            o_ref[:, h * DV:(h + 1) * DV] = o * lax.rsqrt(ms + EPS) * g_ref[...] * (1.0 - lam_init)


def _sample_attention(q, k_new, v_new, cache_k, cache_v, page_table, lam_p, g, layer, n_pg):
    nd, rows, _ = q.shape
    t_new = rows // N_HEADS
    n_pages = page_table.shape[1]
    pt = page_table.reshape(-1)
    page_rows = PAGE_SIZE * N_HEADS

    def page_spec(gi):
        return pl.BlockSpec((None, None, page_rows, DV),
                            lambda b, j, pt_r: (layer, pt_r[b * n_pages + j * n_pg + gi], 0, 0))

    seq = pl.BlockSpec((None, rows, DV), lambda b, j, pt_r: (b, 0, 0))
    vec = lambda w: pl.BlockSpec((None, 1, w), lambda b, j, pt_r: (layer, 0, 0))
    grid_spec = pltpu.PrefetchScalarGridSpec(
        num_scalar_prefetch=1,
        grid=(nd, n_pages // n_pg),
        in_specs=[seq, seq, seq, vec(DK), vec(DK), vec(DK), vec(DK), vec(DV)]
                 + [page_spec(gi) for gi in range(n_pg)] * 2,
        out_specs=pl.BlockSpec((t_new, N_HEADS * DV), lambda b, j, pt_r: (b, 0)),
        scratch_shapes=[
            pltpu.VMEM((N_HEADS, 2 * t_new, DV), BF16),
            pltpu.VMEM((N_HEADS * 2 * t_new, 1), F32),
            pltpu.VMEM((N_HEADS * 2 * t_new, 1), F32),
            pltpu.VMEM((N_HEADS * 2 * t_new, DV), F32),
            pltpu.VMEM((2, N_HEADS, PAGE_SIZE, DV), F32),
        ])
    lq1, lk1, lq2, lk2 = lam_p
    return pl.pallas_call(
        functools.partial(_sattn_body, n_pg=n_pg, t_new=t_new, lam_init=_lambda_init(layer)),
        grid_spec=grid_spec,
        out_shape=jax.ShapeDtypeStruct((nd * t_new, N_HEADS * DV), F32),
        compiler_params=_params(2), name="sample_attention")(
            pt, q, k_new, v_new, lq1, lk1, lq2, lk2, g, *([cache_k] * n_pg), *([cache_v] * n_pg))


def _conv_body(zb_ref, zbh_ref, zc_ref, zch_ref, gb_ref, wb_ref, bb_ref, ng_ref, nb_ref, wc_ref,
               ob_ref, oc_ref, winb_scr, winc_scr, cb_scr, *, tt, tiles_per_seq, zero_start, rc):
    i = pl.program_id(0)
    n_slab = zb_ref.shape[1] // 128
    if zero_start:
        keep = jnp.where(i % tiles_per_seq == 0, 0.0, 1.0)
    else:
        keep = 1.0
    for c in range(n_slab):
        lanes = slice(c * 128, (c + 1) * 128)
        winb_scr[c, 0:HALO_B, :] = zbh_ref[:, lanes] * keep
        winb_scr[c, HALO_B:HALO_B + tt, :] = zb_ref[:, lanes]
        winc_scr[c, 0:HALO_C, :] = zch_ref[:, lanes] * keep
        winc_scr[c, HALO_C:HALO_C + tt, :] = zc_ref[:, lanes]

    for c in range(n_slab):
        lanes = slice(c * 128, (c + 1) * 128)

        def chunk(r, carry):
            r0 = pl.multiple_of(r * rc, rc)
            acc = jnp.zeros((rc, 128), F32)
            for k in range(CONV_B):
                acc = acc + wb_ref[k:k + 1, lanes] * winb_scr[c, pl.ds(r0 + HALO_B - (CONV_B - 1) + k, rc), :]
            cb_scr[pl.ds(r0, rc), lanes] = acc
            acc_c = jnp.zeros((rc, 128), F32)
            for k in range(CONV_C):
                acc_c = acc_c + wc_ref[k:k + 1, lanes] * winc_scr[c, pl.ds(r0 + HALO_C - (CONV_C - 1) + k, rc), :]
            oc_ref[pl.ds(r0, rc), lanes] = (gb_ref[pl.ds(r0, rc), lanes] * acc_c).astype(oc_ref.dtype)
            return carry

        lax.fori_loop(0, tt // rc, chunk, 0)

    y = _layer_norm(cb_scr[...] + bb_ref[...], ng_ref[...], nb_ref[...])
    ob_ref[...] = (y * _sigmoid(y)).astype(ob_ref.dtype)


def _conv_call(zb, zb_halo_src, zc, zc_halo_src, gate_b, wb, bb, ng, nb_, wc, layer, tt, seq_len,
               zero_start, out_dtype):
    n, c = zb.shape
    tiles_per_seq = max(seq_len // tt, 1)
    rc = min(tt, 32)
    if zero_start:
        hb_map = lambda i: (jnp.maximum(i * (tt // HALO_B) - 1, 0), 0)
        hc_map = lambda i: (jnp.maximum(i * (tt // HALO_C) - 1, 0), 0)
    else:
        hb_map = lambda i: (i, 0)
        hc_map = lambda i: (i, 0)
    main = pl.BlockSpec((tt, c), lambda i: (i, 0))
    vec = pl.BlockSpec((None, 1, c), lambda i: (layer, 0, 0))
    return pl.pallas_call(
        functools.partial(_conv_body, tt=tt, tiles_per_seq=tiles_per_seq, zero_start=zero_start, rc=rc),
        grid=(n // tt,),
        in_specs=[main, pl.BlockSpec((HALO_B, c), hb_map), main, pl.BlockSpec((HALO_C, c), hc_map), main,
                  pl.BlockSpec((None, HALO_B, c), lambda i: (layer, 0, 0)), vec, vec, vec,
                  pl.BlockSpec((None, 8, c), lambda i: (layer, 0, 0))],
        out_specs=[main, main],
        out_shape=[jax.ShapeDtypeStruct((n, c), out_dtype)] * 2,
        scratch_shapes=[pltpu.VMEM((c // 128, HALO_B + tt, 128), F32),
                        pltpu.VMEM((c // 128, HALO_C + tt, 128), F32),
                        pltpu.VMEM((tt, c), F32)],
        compiler_params=_params(1), name="branch_convs")(
            zb, zb_halo_src, zc, zc_halo_src, gate_b, wb, bb, ng, nb_, wc)


def _merge_body(x_ref, oa_ref, ob_ref, oc_ref, g0_ref, g1_ref, g2_ref, wbr_ref, wo_ref, lg_ref, lb_ref,
                y_ref, *, alpha):
    merged = g0_ref[...] * jnp.dot(oa_ref[...].astype(BF16), wbr_ref[0], preferred_element_type=F32)
    merged = merged + g1_ref[...] * jnp.dot(ob_ref[...].astype(BF16), wbr_ref[1], preferred_element_type=F32)
    merged = merged + g2_ref[...] * jnp.dot(oc_ref[...].astype(BF16), wbr_ref[2], preferred_element_type=F32)
    y = jnp.dot(merged.astype(BF16), wo_ref[...], preferred_element_type=F32)
    y_ref[...] = _layer_norm(alpha * x_ref[...] + y, lg_ref[...], lb_ref[...])


def _merge_call(x, oa, ob, oc, gates, w_br, w_o, ln_g, ln_b, layer, tm):
    n, d = x.shape
    row = pl.BlockSpec((tm, d), lambda i: (i, 0))
    vec = pl.BlockSpec((None, 1, d), lambda i: (layer, 0, 0))
    once = pl.Buffered(1)
    return pl.pallas_call(
        functools.partial(_merge_body, alpha=_alpha()),
        grid=(n // tm,),
        in_specs=[row] * 7 + [
            pl.BlockSpec((None, N_BRANCH, d, d), lambda i: (layer, 0, 0, 0), pipeline_mode=once),
            pl.BlockSpec((None, d, d), lambda i: (layer, 0, 0), pipeline_mode=once), vec, vec],
        out_specs=row,
        out_shape=jax.ShapeDtypeStruct((n, d), F32),
        compiler_params=_params(1), name="merge")(x, oa, ob, oc, *gates, w_br, w_o, ln_g, ln_b)


FF_CHUNK = 256


def _ffn_body(x_ref, *rest, alpha, tm, seq_len, tiles_per_seq):
    carried = seq_len >= tm
    if carried:
        wup_ref, wdn_ref, cw_ref, cbias_ref, lg_ref, lb_ref, y_ref, tail_ref, gbuf_scr, carry_scr = rest
    else:
        (s1_ref, s2_ref, wup_ref, wdn_ref, cw_ref, cbias_ref, lg_ref, lb_ref, y_ref, tail_ref,
         gbuf_scr) = rest
    i = pl.program_id(0)
    x = x_ref[...]
    xb = x.astype(BF16)
    acc = jnp.zeros((tm, x.shape[1]), F32)

    if carried:
        @pl.when(i % tiles_per_seq == 0)
        def _():
            carry_scr[...] = jnp.zeros(carry_scr.shape, F32)
    else:
        pos = lax.broadcasted_iota(jnp.int32, (tm, FF_CHUNK), 0) % seq_len
        gbuf_scr[0:8, :] = jnp.zeros((8, FF_CHUNK), F32)

    for c in range(D_FF // FF_CHUNK):
        cols = slice(c * FF_CHUNK, (c + 1) * FF_CHUNK)
        gcols = slice(D_FF + c * FF_CHUNK, D_FF + (c + 1) * FF_CHUNK)
        u = jnp.dot(xb, wup_ref[:, cols], preferred_element_type=F32)
        g = jnp.dot(xb, wup_ref[:, gcols], preferred_element_type=F32)
        if carried:
            gbuf_scr[0:8, :] = carry_scr[:, cols]
        gbuf_scr[8:8 + tm, :] = g
        g1 = gbuf_scr[pl.ds(7, tm), :]
        g2 = gbuf_scr[pl.ds(6, tm), :]
        if carried:
            carry_scr[:, cols] = g[tm - 8:tm, :]
            tail_ref[:, cols] = g[tm - 8:tm, :]
        else:
            g1 = jnp.where(pos >= 1, g1, 0.0) + s1_ref[:, cols]
            g2 = jnp.where(pos >= 2, g2, 0.0) + s2_ref[:, cols]
            tail_ref[:, cols] = g
        cf = cw_ref[0:1, cols] * g2 + cw_ref[1:2, cols] * g1 + cw_ref[2:3, cols] * g + cbias_ref[:, cols]
        act = cf * _sigmoid(cf) * u
        acc = acc + jnp.dot(act.astype(BF16), wdn_ref[cols, :], preferred_element_type=F32)

    y_ref[...] = _layer_norm(alpha * x + acc, lg_ref[...], lb_ref[...])


def _ffn_call(x, w_up, w_dn, cw, cbias, ln_g, ln_b, layer, tm, seq_len, inject=None):
    n, d = x.shape
    carried = seq_len >= tm
    tiles_per_seq = max(seq_len // tm, 1)
    row = pl.BlockSpec((tm, d), lambda i: (i, 0))
    vec = pl.BlockSpec((None, 1, d), lambda i: (layer, 0, 0))
    once = pl.Buffered(1)
    in_specs = [row]
    args = [x]
    if not carried:
        wide = pl.BlockSpec((tm, D_FF), lambda i: (i, 0))
        in_specs += [wide, wide]
        args += list(inject)
    in_specs += [pl.BlockSpec((None, d, 2 * D_FF), lambda i: (layer, 0, 0), pipeline_mode=once),
                 pl.BlockSpec((None, D_FF, d), lambda i: (layer, 0, 0), pipeline_mode=once),
                 pl.BlockSpec((None, 8, D_FF), lambda i: (layer, 0, 0)),
                 pl.BlockSpec((None, 1, D_FF), lambda i: (layer, 0, 0)), vec, vec]
    args += [w_up, w_dn, cw, cbias, ln_g, ln_b]
    scratch = [pltpu.VMEM((8 + tm, FF_CHUNK), F32)]
    if carried:
        scratch.append(pltpu.VMEM((8, D_FF), F32))
        tail_shape = jax.ShapeDtypeStruct((n // tm, 8, D_FF), F32)
        tail_spec = pl.BlockSpec((None, 8, D_FF), lambda i: (i, 0, 0))
    else:
        tail_shape = jax.ShapeDtypeStruct((n, D_FF), F32)
        tail_spec = pl.BlockSpec((tm, D_FF), lambda i: (i, 0))
    return pl.pallas_call(
        functools.partial(_ffn_body, alpha=_alpha(), tm=tm, seq_len=seq_len, tiles_per_seq=tiles_per_seq),
        grid=(n // tm,), in_specs=in_specs, out_specs=[row, tail_spec],
        out_shape=[jax.ShapeDtypeStruct((n, d), F32), tail_shape],
        scratch_shapes=scratch, compiler_params=_params(1), name="conv_ffn")(*args)


def _pad_rows_front(a, rows):
    return jnp.pad(a, ((0, 0), (0, 0), (rows - a.shape[2], 0), (0, 0)))


def _layer(x, layer, wts, nseq, seq_len, prompt, cfg, sample_in=None):
    (w_in, lam_p, attn_g, cb_w, cb_bias, cb_ng, cb_nb, cc_w, w_br, w_o, ln1_g, ln1_b, w_up, cf_w, cf_bias,
     w_dn, ln2_g, ln2_b) = wts
    d = D_MODEL
    qk_w = N_HEADS * 2 * DK
    o_b0 = 2 * qk_w + N_HEADS * DV
    o_c0 = o_b0 + 2 * d
    o_g0 = o_c0 + 3 * d
    tm, tc = cfg["tm"], cfg["tc"]

    zb, zc, gate_b = _proj_call(x, w_in, layer, [o_b0, o_b0 + d, o_c0, o_c0 + d, o_c0 + 2 * d], d,
                                _bc_epilogue, [F32] * 3, tm, tc, "proj_bc")
    gates = _proj_call(x, w_in, layer, [o_g0, o_g0 + d, o_g0 + 2 * d], d, _gates_epilogue, [F32] * 3,
                       tm, tc, "proj_gates")

    if prompt:
        q, k, v, kb, vb = _proj_call(x, w_in, layer, [0, qk_w, 2 * qk_w], d, _qkv_prompt_epilogue,
                                     [BF16, F32, F32, BF16, BF16], tm, tc, "proj_qkv")
        o_a = _prompt_attention(q, kb, vb, lam_p, attn_g, layer, nseq, seq_len, cfg["tq"], cfg["tk"])
        o_b, o_c = _conv_call(zb, zb, zc, zc, gate_b, cb_w, cb_bias, cb_ng, cb_nb, cc_w, layer,
                              cfg["tt"], seq_len, True, BF16)
    else:
        cache_k, cache_v, page_table, st_b, st_c, st_f = sample_in
        q, k, v = _proj_call(x, w_in, layer, [0, qk_w, 2 * qk_w], d, _qkv_sample_epilogue, [F32] * 3,
                             tm, tc, "proj_qkv")
        rows = seq_len * N_HEADS
        to_rows = lambda a: a.reshape(nseq, rows, DV)
        o_a = _sample_attention(to_rows(q), to_rows(k), to_rows(v), cache_k, cache_v, page_table, lam_p,
                                attn_g, layer, cfg["n_pg"])
        o_b, o_c = _conv_call(zb, st_b[layer].reshape(nseq * HALO_B, d), zc,
                              st_c[layer].reshape(nseq * HALO_C, d), gate_b, cb_w, cb_bias, cb_ng, cb_nb,
                              cc_w, layer, seq_len, seq_len, False, F32)

    x1 = _merge_call(x, o_a, o_b, o_c, gates, w_br, w_o, ln1_g, ln1_b, layer, cfg["tm_merge"])

    if prompt:
        x2, tail = _ffn_call(x1, w_up, w_dn, cf_w, cf_bias, ln2_g, ln2_b, layer, cfg["tm_ffn"], seq_len)
        tps = seq_len // cfg["tm_ffn"]
        new_f = tail.reshape(nseq, tps, 8, D_FF)[:, tps - 1, 8 - (CONV_F - 1):, :]
        new_b = zb.reshape(nseq, seq_len, d)[:, seq_len - (CONV_B - 1):, :]
        new_c = zc.reshape(nseq, seq_len, d)[:, seq_len - (CONV_C - 1):, :]
    else:
        sf = st_f[layer]
        zeros = jnp.zeros((nseq, seq_len, D_FF), F32)
        s1 = zeros.at[:, 0].set(sf[:, 1]).reshape(nseq * seq_len, D_FF)
        s2 = zeros.at[:, 0].set(sf[:, 0]).at[:, 1].set(sf[:, 1]).reshape(nseq * seq_len, D_FF)
        x2, g_f = _ffn_call(x1, w_up, w_dn, cf_w, cf_bias, ln2_g, ln2_b, layer, nseq * seq_len, seq_len,
                            inject=(s1, s2))
        new_f = g_f.reshape(nseq, seq_len, D_FF)[:, seq_len - (CONV_F - 1):, :]
        new_b = jnp.concatenate([st_b[layer][:, HALO_B - (CONV_B - 1) + seq_len:, :],
                                 zb.reshape(nseq, seq_len, d)], axis=1)
        new_c = zc.reshape(nseq, seq_len, d)[:, seq_len - (CONV_C - 1):, :]
    return x2, k, v, new_b, new_c, new_f


PROMPT_CFG = dict(tm=512, tc=512, tq=1024, tk=512, tt=256, tm_merge=256, tm_ffn=256)
SAMPLE_CFG = dict(tm=256, tc=512, n_pg=8, tm_merge=256)


def kernel(x_prompt, x_sample, cache_k, cache_v, state_conv_b, state_conv_c, state_conv_ffn, page_table, w_in,
           lambda_q1, lambda_k1, lambda_q2, lambda_k2, attn_norm_g, conv_b_w, conv_b_bias, conv_b_norm_g,
           conv_b_norm_b, conv_c_w, w_branch, w_o, ln1_g, ln1_b, w_up, conv_f_w, conv_f_bias, w_down, ln2_g,
           ln2_b):
    nbp, s_len, d = x_prompt.shape
    nd, t_new, _ = x_sample.shape
    depth = w_in.shape[0]
    row3 = lambda a: a.reshape(depth, 1, a.shape[-1])
    pad8 = lambda a: jnp.pad(a, ((0, 0), (0, 8 - a.shape[1] % 8 if a.shape[1] % 8 else 0), (0, 0)))
    wts = (w_in.astype(BF16),
           (row3(lambda_q1), row3(lambda_k1), row3(lambda_q2), row3(lambda_k2)),
           row3(attn_norm_g), pad8(conv_b_w), row3(conv_b_bias), row3(conv_b_norm_g), row3(conv_b_norm_b),
           pad8(conv_c_w), w_branch.astype(BF16), w_o.astype(BF16), row3(ln1_g), row3(ln1_b),
           w_up.astype(BF16), pad8(conv_f_w), row3(conv_f_bias), w_down.astype(BF16), row3(ln2_g), row3(ln2_b))
    pool = cache_k.shape[1]
    ck = cache_k.reshape(depth, pool, PAGE_SIZE * N_HEADS, 2 * DK)
    cv = cache_v.reshape(depth, pool, PAGE_SIZE * N_HEADS, DV)
    sample_in = (ck, cv, page_table, _pad_rows_front(state_conv_b, HALO_B),
                 _pad_rows_front(state_conv_c, HALO_C), state_conv_ffn)

    xp = x_prompt.reshape(nbp * s_len, d)
    xs = x_sample.reshape(nd * t_new, d)
    outs = [[] for _ in range(10)]
    for layer in range(depth):
        xp, k, v, nb_, nc_, nf_ = _layer(xp, layer, wts, nbp, s_len, True, PROMPT_CFG)
        outs[0].append(k.reshape(-1, PAGE_SIZE, N_HEADS, 2 * DK))
        outs[1].append(v.reshape(-1, PAGE_SIZE, N_HEADS, DV))
        outs[2].append(nb_)
        outs[3].append(nc_)
        outs[4].append(nf_)
        xs, k, v, nb_, nc_, nf_ = _layer(xs, layer, wts, nd, t_new, False, SAMPLE_CFG, sample_in)
        outs[5].append(k.reshape(nd, t_new, N_HEADS, 2 * DK))
        outs[6].append(v.reshape(nd, t_new, N_HEADS, DV))
        outs[7].append(nb_)
        outs[8].append(nc_)
        outs[9].append(nf_)
    return (xp.reshape(nbp, s_len, d), xs.reshape(nd, t_new, d)) + tuple(jnp.stack(o) for o in outs)
```

```python
import functools
import math

import numpy as np
import jax
import jax.numpy as jnp
from jax import lax
from jax.experimental import pallas as pl
from jax.experimental.pallas import tpu as pltpu

F32 = jnp.float32
BF16 = jnp.bfloat16

D_MODEL = 1024
DEPTH = 2
PAST_LEN = 16384
PAGE_SIZE = 128
N_HEADS = 8
DK = 64
DV = 2 * DK
N_BRANCH = 3
CONV_B = 31
CONV_C = 3
CONV_F = 3
D_FF = 2816
EPS = 1e-5
LOG2E = math.log2(math.e)
NEG_BIG = -1e30

VMEM_LIMIT_BYTES = 52 * 1024 * 1024
HALO_B = 32
HALO_C = 8


def _alpha():
    return (2 * DEPTH) ** 0.25


def _lambda_init(layer):
    return 0.8 - 0.6 * math.exp(-0.3 * layer)


def _params(n_axes):
    return pltpu.CompilerParams(dimension_semantics=("arbitrary",) * n_axes,
                                vmem_limit_bytes=VMEM_LIMIT_BYTES)


def _sigmoid(x):
    return 1.0 / (1.0 + jnp.exp(-x))


def _layer_norm(x, g, b):
    mu = jnp.mean(x, axis=-1, keepdims=True)
    xc = x - mu
    var = jnp.mean(xc * xc, axis=-1, keepdims=True)
    return xc * lax.rsqrt(var + EPS) * g + b


def _lam_value(lq1_ref, lk1_ref, lq2_ref, lk2_ref, lam_init):
    a = jnp.sum(lq1_ref[...] * lk1_ref[...], axis=-1, keepdims=True)
    b = jnp.sum(lq2_ref[...] * lk2_ref[...], axis=-1, keepdims=True)
    return jnp.exp(a) - jnp.exp(b) + lam_init


def _slope_log2(h):
    hv = jnp.full((1, 1), h + 1, jnp.int32).astype(F32)
    return jnp.exp2(-hv) * LOG2E


def _proj_body(x_ref, *refs, n_groups, epilogue):
    w_refs = refs[:n_groups]
    out_refs = refs[n_groups:]
    xb = x_ref[...].astype(BF16)
    hs = [jnp.dot(xb, w[...], preferred_element_type=F32) for w in w_refs]
    epilogue(hs, out_refs)


def _proj_call(x, w, layer, col_starts, width, epilogue, out_dtypes, tm, tc, name):
    n, d = x.shape
    n_groups = len(col_starts)
    grid = (width // tc, n // tm)
    in_specs = [pl.BlockSpec((tm, d), lambda c, i: (i, 0))]
    for cs in col_starts:
        in_specs.append(pl.BlockSpec((None, d, tc), lambda c, i, cb=cs // tc: (layer, 0, cb + c)))
    out_specs = [pl.BlockSpec((tm, tc), lambda c, i: (i, c)) for _ in out_dtypes]
    out_shape = [jax.ShapeDtypeStruct((n, width), dt) for dt in out_dtypes]
    return pl.pallas_call(
        functools.partial(_proj_body, n_groups=n_groups, epilogue=epilogue),
        grid=grid, in_specs=in_specs, out_specs=out_specs, out_shape=out_shape,
        compiler_params=_params(2), name=name)(x, *([w] * n_groups))


def _qkv_prompt_epilogue(hs, outs):
    q, k, v = hs
    q_ref, k_ref, v_ref, kb_ref, vb_ref = outs
    q_ref[...] = (q * (DK ** -0.5 * LOG2E)).astype(BF16)
    k_ref[...] = k
    v_ref[...] = v
    kb_ref[...] = k.astype(BF16)
    vb_ref[...] = v.astype(BF16)


def _qkv_sample_epilogue(hs, outs):
    q, k, v = hs
    q_ref, k_ref, v_ref = outs
    q_ref[...] = q * (DK ** -0.5 * LOG2E)
    k_ref[...] = k
    v_ref[...] = v


def _bc_epilogue(hs, outs):
    a_b, g_b, gate_b, gate_c, h_c = hs
    zb_ref, zc_ref, gb_ref = outs
    zb_ref[...] = a_b * _sigmoid(g_b)
    zc_ref[...] = gate_c * h_c
    gb_ref[...] = gate_b


def _gates_epilogue(hs, outs):
    for h, o in zip(hs, outs):
        o[...] = _sigmoid(h)


def _pattn_body(qi_ref, kj_ref, q_ref, k_ref, vt_ref, lq1_ref, lk1_ref, lq2_ref, lk2_ref, g_ref,
                o_ref, qm_scr, m_scr, l_scr, acc_scr, rel_scr, bias_scr, *, tq, tk, lam_init):
    h = pl.program_id(1)
    s = pl.program_id(2)
    i = qi_ref[s]
    j = kj_ref[s]
    slope = _slope_log2(h)

    @pl.when(s == 0)
    def _():
        rel = (lax.broadcasted_iota(jnp.int32, (tk, tq), 0)
               - lax.broadcasted_iota(jnp.int32, (tk, tq), 1)).astype(F32)
        rel_scr[...] = rel
        bias_scr[...] = rel * slope

    @pl.when(j == 0)
    def _():
        q = q_ref[...]
        lane = lax.broadcasted_iota(jnp.int32, q.shape, 1)
        zero = jnp.zeros_like(q)
        qm_scr[0] = jnp.where(lane < DK, q, zero)
        qm_scr[1] = jnp.where(lane >= DK, q, zero)
        m_scr[...] = jnp.full(m_scr.shape, NEG_BIG, F32)
        l_scr[...] = jnp.zeros(l_scr.shape, F32)
        acc_scr[...] = jnp.zeros(acc_scr.shape, F32)

    off_i = j * tk - i * tq
    off = jnp.full((1, 1), off_i, jnp.int32).astype(F32)
    c_bias = off * slope

    def step(masked):
        k = k_ref[...]
        vt = vt_ref[...]
        for mp in range(2):
            sc = lax.dot_general(k, qm_scr[mp], (((1,), (1,)), ((), ())),
                                 preferred_element_type=F32) + bias_scr[...]
            if masked:
                sc = jnp.where(rel_scr[...] + off <= 0.0, sc, NEG_BIG)
            m_prev = m_scr[mp]
            m_new = jnp.maximum(m_prev, jnp.max(sc, axis=0, keepdims=True) + c_bias)
            alpha = jnp.exp2(m_prev - m_new)
            p = jnp.exp2(sc - (m_new - c_bias))
            l_scr[mp] = alpha * l_scr[mp] + jnp.sum(p, axis=0, keepdims=True)
            acc_scr[mp] = alpha * acc_scr[mp] + jnp.dot(vt, p.astype(BF16), preferred_element_type=F32)
            m_scr[mp] = m_new

    needs_mask = off_i + tk - 1 > 0

    @pl.when(needs_mask)
    def _():
        step(True)

    @pl.when(jnp.logical_not(needs_mask))
    def _():
        step(False)

    @pl.when(j == ((i + 1) * tq) // tk - 1)
    def _():
        lam = _lam_value(lq1_ref, lk1_ref, lq2_ref, lk2_ref, lam_init)
        ot = acc_scr[0] / l_scr[0] - lam * (acc_scr[1] / l_scr[1])
        ms = jnp.mean(ot * ot, axis=0, keepdims=True)
        o = (ot * lax.rsqrt(ms + EPS)).T
        o_ref[...] = (o * g_ref[...] * (1.0 - lam_init)).astype(o_ref.dtype)


def _prompt_attention(q, kb, vb, lam_p, g, layer, nb, s_len, tq, tk):
    hw = N_HEADS * DV
    q3 = q.reshape(nb, s_len, hw)
    k3 = kb.reshape(nb, s_len, hw)
    vt = jnp.transpose(vb.reshape(nb, s_len, N_HEADS, DV), (0, 2, 3, 1))
    qi, kj = [], []
    for i in range(s_len // tq):
        for j in range(((i + 1) * tq) // tk):
            qi.append(i)
            kj.append(j)
    n_steps = len(qi)
    qi = jnp.asarray(np.asarray(qi, np.int32))
    kj = jnp.asarray(np.asarray(kj, np.int32))
    vec = lambda w: pl.BlockSpec((None, 1, w), lambda b, h, s, qi_r, kj_r: (layer, 0, 0))
    grid_spec = pltpu.PrefetchScalarGridSpec(
        num_scalar_prefetch=2,
        grid=(nb, N_HEADS, n_steps),
        in_specs=[
            pl.BlockSpec((None, tq, DV), lambda b, h, s, qi_r, kj_r: (b, qi_r[s], h)),
            pl.BlockSpec((None, tk, DV), lambda b, h, s, qi_r, kj_r: (b, kj_r[s], h)),
            pl.BlockSpec((None, None, DV, tk), lambda b, h, s, qi_r, kj_r: (b, h, 0, kj_r[s])),
            vec(DK), vec(DK), vec(DK), vec(DK), vec(DV),
        ],
        out_specs=pl.BlockSpec((None, tq, DV), lambda b, h, s, qi_r, kj_r: (b, qi_r[s], h)),
        scratch_shapes=[
            pltpu.VMEM((2, tq, DV), BF16),
            pltpu.VMEM((2, 1, tq), F32),
            pltpu.VMEM((2, 1, tq), F32),
            pltpu.VMEM((2, DV, tq), F32),
            pltpu.VMEM((tk, tq), F32),
            pltpu.VMEM((tk, tq), F32),
        ])
    lq1, lk1, lq2, lk2 = lam_p
    o = pl.pallas_call(
        functools.partial(_pattn_body, tq=tq, tk=tk, lam_init=_lambda_init(layer)),
        grid_spec=grid_spec,
        out_shape=jax.ShapeDtypeStruct((nb, s_len, hw), BF16),
        compiler_params=_params(3), name="prompt_attention")(qi, kj, q3, k3, vt, lq1, lk1, lq2, lk2, g)
    return o.reshape(nb * s_len, hw)


def _sattn_body(pt_ref, q_ref, kn_ref, vn_ref, lq1_ref, lk1_ref, lq2_ref, lk2_ref, g_ref, *rest,
                n_pg, t_new, lam_init):
    k_refs = rest[:n_pg]
    v_refs = rest[n_pg:2 * n_pg]
    o_ref = rest[2 * n_pg]
    qm_scr, m_scr, l_scr, acc_scr, pad_scr = rest[2 * n_pg + 1:]
    j = pl.program_id(1)
    nj = pl.num_programs(1)
    nr = 2 * t_new
    nrows = N_HEADS * nr
    nk = n_pg * PAGE_SIZE

    ridx = lax.broadcasted_iota(jnp.int32, (nrows, 1), 0)
    head1 = (lax.shift_right_logical(ridx, int(math.log2(nr))) + 1).astype(F32)
    slope = jnp.exp2(-head1) * LOG2E
    tok = jnp.bitwise_and(ridx, t_new - 1)

    def scores(k_heads):
        return jnp.concatenate(
            [lax.dot_general(qm_scr[h], k_heads[h], (((1,), (1,)), ((), ())), preferred_element_type=F32)
             for h in range(N_HEADS)], axis=0)

    def update(sc, v_heads):
        m_prev = m_scr[...]
        m_new = jnp.maximum(m_prev, jnp.max(sc, axis=-1, keepdims=True))
        alpha = jnp.exp2(m_prev - m_new)
        p = jnp.exp2(sc - m_new)
        l_scr[...] = alpha * l_scr[...] + jnp.sum(p, axis=-1, keepdims=True)
        pb = p.astype(BF16)
        pv = jnp.concatenate(
            [jnp.dot(pb[h * nr:(h + 1) * nr], v_heads[h], preferred_element_type=F32)
             for h in range(N_HEADS)], axis=0)
        acc_scr[...] = alpha * acc_scr[...] + pv
        m_scr[...] = m_new

    @pl.when(j == 0)
    def _():
        m_scr[...] = jnp.full(m_scr.shape, NEG_BIG, F32)
        l_scr[...] = jnp.zeros(l_scr.shape, F32)
        acc_scr[...] = jnp.zeros(acc_scr.shape, F32)
        pad_scr[...] = jnp.zeros(pad_scr.shape, F32)
        lane = lax.broadcasted_iota(jnp.int32, (t_new, DV), 1)
        for h in range(N_HEADS):
            q = q_ref[pl.ds(h, t_new, stride=N_HEADS), :]
            zero = jnp.zeros_like(q)
            qm_scr[h] = jnp.concatenate([jnp.where(lane < DK, q, zero), jnp.where(lane >= DK, q, zero)],
                                        axis=0).astype(BF16)
            pad_scr[0, h, 0:t_new, :] = kn_ref[pl.ds(h, t_new, stride=N_HEADS), :]
            pad_scr[1, h, 0:t_new, :] = vn_ref[pl.ds(h, t_new, stride=N_HEADS), :]
        col = lax.broadcasted_iota(jnp.int32, (1, PAGE_SIZE), 1)
        sc = scores([pad_scr[0, h].astype(BF16) for h in range(N_HEADS)])
        sc = sc + (col - tok).astype(F32) * slope
        sc = jnp.where(col <= tok, sc, NEG_BIG)
        update(sc, [pad_scr[1, h].astype(BF16) for h in range(N_HEADS)])

    def head_rows(refs, h):
        return jnp.concatenate([r[pl.ds(h, PAGE_SIZE, stride=N_HEADS), :] for r in refs], axis=0).astype(BF16)

    colf = lax.broadcasted_iota(jnp.int32, (1, nk), 1).astype(F32)
    base = (jnp.full((1, 1), j * nk - PAST_LEN, jnp.int32) - tok).astype(F32)
    sc = scores([head_rows(k_refs, h) for h in range(N_HEADS)])
    sc = sc + (colf + base) * slope
    update(sc, [head_rows(v_refs, h) for h in range(N_HEADS)])

    @pl.when(j == nj - 1)
    def _():
        lam = _lam_value(lq1_ref, lk1_ref, lq2_ref, lk2_ref, lam_init)
        o_all = acc_scr[...] / l_scr[...]
        for h in range(N_HEADS):
            r0 = h * nr
            o = o_all[r0:r0 + t_new] - lam * o_all[r0 + t_new:r0 + nr]
            ms = jnp.mean(o * o, axis=-1, keepdims=True)
            o_ref[:, h * DV:(h + 1) * DV] = o * lax.rsqrt(ms + EPS) * g_ref[...] * (1.0 - lam_init)


def _sample_attention(q, k_new, v_new, cache_k, cache_v, page_table, lam_p, g, layer, n_pg):
    nd, rows, _ = q.shape
    t_new = rows // N_HEADS
    n_pages = page_table.shape[1]
    pt = page_table.reshape(-1)
    page_rows = PAGE_SIZE * N_HEADS

    def page_spec(gi):
        return pl.BlockSpec((None, None, page_rows, DV),
                            lambda b, j, pt_r: (layer, pt_r[b * n_pages + j * n_pg + gi], 0, 0))

    seq = pl.BlockSpec((None, rows, DV), lambda b, j, pt_r: (b, 0, 0))
    vec = lambda w: pl.BlockSpec((None, 1, w), lambda b, j, pt_r: (layer, 0, 0))
    grid_spec = pltpu.PrefetchScalarGridSpec(
        num_scalar_prefetch=1,
        grid=(nd, n_pages // n_pg),
        in_specs=[seq, seq, seq, vec(DK), vec(DK), vec(DK), vec(DK), vec(DV)]
                 + [page_spec(gi) for gi in range(n_pg)] * 2,
        out_specs=pl.BlockSpec((t_new, N_HEADS * DV), lambda b, j, pt_r: (b, 0)),
        scratch_shapes=[
            pltpu.VMEM((N_HEADS, 2 * t_new, DV), BF16),
            pltpu.VMEM((N_HEADS * 2 * t_new, 1), F32),
            pltpu.VMEM((N_HEADS * 2 * t_new, 1), F32),
            pltpu.VMEM((N_HEADS * 2 * t_new, DV), F32),
            pltpu.VMEM((2, N_HEADS, PAGE_SIZE, DV), F32),
        ])
    lq1, lk1, lq2, lk2 = lam_p
    return pl.pallas_call(
        functools.partial(_sattn_body, n_pg=n_pg, t_new=t_new, lam_init=_lambda_init(layer)),
        grid_spec=grid_spec,
        out_shape=jax.ShapeDtypeStruct((nd * t_new, N_HEADS * DV), F32),
        compiler_params=_params(2), name="sample_attention")(
            pt, q, k_new, v_new, lq1, lk1, lq2, lk2, g, *([cache_k] * n_pg), *([cache_v] * n_pg))


def _conv_body(zb_ref, zbh_ref, zc_ref, zch_ref, gb_ref, wb_ref, bb_ref, ng_ref, nb_ref, wc_ref,
               ob_ref, oc_ref, winb_scr, winc_scr, cb_scr, *, tt, tiles_per_seq, zero_start, rc):
    i = pl.program_id(0)
    n_slab = zb_ref.shape[1] // 128
    if zero_start:
        keep = jnp.where(i % tiles_per_seq == 0, 0.0, 1.0)
    else:
        keep = 1.0
    for c in range(n_slab):
        lanes = slice(c * 128, (c + 1) * 128)
        winb_scr[c, 0:HALO_B, :] = zbh_ref[:, lanes] * keep
        winb_scr[c, HALO_B:HALO_B + tt, :] = zb_ref[:, lanes]
        winc_scr[c, 0:HALO_C, :] = zch_ref[:, lanes] * keep
        winc_scr[c, HALO_C:HALO_C + tt, :] = zc_ref[:, lanes]

    for c in range(n_slab):
        lanes = slice(c * 128, (c + 1) * 128)

        def chunk(r, carry):
            r0 = pl.multiple_of(r * rc, rc)
            acc = jnp.zeros((rc, 128), F32)
            for k in range(CONV_B):
                acc = acc + wb_ref[k:k + 1, lanes] * winb_scr[c, pl.ds(r0 + HALO_B - (CONV_B - 1) + k, rc), :]
            cb_scr[pl.ds(r0, rc), lanes] = acc
            acc_c = jnp.zeros((rc, 128), F32)
            for k in range(CONV_C):
                acc_c = acc_c + wc_ref[k:k + 1, lanes] * winc_scr[c, pl.ds(r0 + HALO_C - (CONV_C - 1) + k, rc), :]
            oc_ref[pl.ds(r0, rc), lanes] = (gb_ref[pl.ds(r0, rc), lanes] * acc_c).astype(oc_ref.dtype)
            return carry

        lax.fori_loop(0, tt // rc, chunk, 0)

    y = _layer_norm(cb_scr[...] + bb_ref[...], ng_ref[...], nb_ref[...])
    ob_ref[...] = (y * _sigmoid(y)).astype(ob_ref.dtype)


def _conv_call(zb, zb_halo_src, zc, zc_halo_src, gate_b, wb, bb, ng, nb_, wc, layer, tt, seq_len,
               zero_start, out_dtype):
    n, c = zb.shape
    tiles_per_seq = max(seq_len // tt, 1)
    rc = min(tt, 32)
    if zero_start:
        hb_map = lambda i: (jnp.maximum(i * (tt // HALO_B) - 1, 0), 0)
        hc_map = lambda i: (jnp.maximum(i * (tt // HALO_C) - 1, 0), 0)
    else:
        hb_map = lambda i: (i, 0)
        hc_map = lambda i: (i, 0)
    main = pl.BlockSpec((tt, c), lambda i: (i, 0))
    vec = pl.BlockSpec((None, 1, c), lambda i: (layer, 0, 0))
    return pl.pallas_call(
        functools.partial(_conv_body, tt=tt, tiles_per_seq=tiles_per_seq, zero_start=zero_start, rc=rc),
        grid=(n // tt,),
        in_specs=[main, pl.BlockSpec((HALO_B, c), hb_map), main, pl.BlockSpec((HALO_C, c), hc_map), main,
                  pl.BlockSpec((None, HALO_B, c), lambda i: (layer, 0, 0)), vec, vec, vec,
                  pl.BlockSpec((None, 8, c), lambda i: (layer, 0, 0))],
        out_specs=[main, main],
        out_shape=[jax.ShapeDtypeStruct((n, c), out_dtype)] * 2,
        scratch_shapes=[pltpu.VMEM((c // 128, HALO_B + tt, 128), F32),
                        pltpu.VMEM((c // 128, HALO_C + tt, 128), F32),
                        pltpu.VMEM((tt, c), F32)],
        compiler_params=_params(1), name="branch_convs")(
            zb, zb_halo_src, zc, zc_halo_src, gate_b, wb, bb, ng, nb_, wc)


def _merge_body(x_ref, oa_ref, ob_ref, oc_ref, g0_ref, g1_ref, g2_ref, wbr_ref, wo_ref, lg_ref, lb_ref,
                y_ref, *, alpha):
    merged = g0_ref[...] * jnp.dot(oa_ref[...].astype(BF16), wbr_ref[0], preferred_element_type=F32)
    merged = merged + g1_ref[...] * jnp.dot(ob_ref[...].astype(BF16), wbr_ref[1], preferred_element_type=F32)
    merged = merged + g2_ref[...] * jnp.dot(oc_ref[...].astype(BF16), wbr_ref[2], preferred_element_type=F32)
    y = jnp.dot(merged.astype(BF16), wo_ref[...], preferred_element_type=F32)
    y_ref[...] = _layer_norm(alpha * x_ref[...] + y, lg_ref[...], lb_ref[...])


def _merge_call(x, oa, ob, oc, gates, w_br, w_o, ln_g, ln_b, layer, tm):
    n, d = x.shape
    row = pl.BlockSpec((tm, d), lambda i: (i, 0))
    vec = pl.BlockSpec((None, 1, d), lambda i: (layer, 0, 0))
    once = pl.Buffered(1)
    return pl.pallas_call(
        functools.partial(_merge_body, alpha=_alpha()),
        grid=(n // tm,),
        in_specs=[row] * 7 + [
            pl.BlockSpec((None, N_BRANCH, d, d), lambda i: (layer, 0, 0, 0), pipeline_mode=once),
            pl.BlockSpec((None, d, d), lambda i: (layer, 0, 0), pipeline_mode=once), vec, vec],
        out_specs=row,
        out_shape=jax.ShapeDtypeStruct((n, d), F32),
        compiler_params=_params(1), name="merge")(x, oa, ob, oc, *gates, w_br, w_o, ln_g, ln_b)


FF_CHUNK = 256


def _ffn_body(x_ref, *rest, alpha, tm, seq_len, tiles_per_seq):
    carried = seq_len >= tm
    if carried:
        wup_ref, wdn_ref, cw_ref, cbias_ref, lg_ref, lb_ref, y_ref, tail_ref, gbuf_scr, carry_scr = rest
    else:
        (s1_ref, s2_ref, wup_ref, wdn_ref, cw_ref, cbias_ref, lg_ref, lb_ref, y_ref, tail_ref,
         gbuf_scr) = rest
    i = pl.program_id(0)
    x = x_ref[...]
    xb = x.astype(BF16)
    acc = jnp.zeros((tm, x.shape[1]), F32)

    if carried:
        @pl.when(i % tiles_per_seq == 0)
        def _():
            carry_scr[...] = jnp.zeros(carry_scr.shape, F32)
    else:
        pos = lax.broadcasted_iota(jnp.int32, (tm, FF_CHUNK), 0) % seq_len
        gbuf_scr[0:8, :] = jnp.zeros((8, FF_CHUNK), F32)

    for c in range(D_FF // FF_CHUNK):
        cols = slice(c * FF_CHUNK, (c + 1) * FF_CHUNK)
        gcols = slice(D_FF + c * FF_CHUNK, D_FF + (c + 1) * FF_CHUNK)
        u = jnp.dot(xb, wup_ref[:, cols], preferred_element_type=F32)
        g = jnp.dot(xb, wup_ref[:, gcols], preferred_element_type=F32)
        if carried:
            gbuf_scr[0:8, :] = carry_scr[:, cols]
        gbuf_scr[8:8 + tm, :] = g
        g1 = gbuf_scr[pl.ds(7, tm), :]
        g2 = gbuf_scr[pl.ds(6, tm), :]
        if carried:
            carry_scr[:, cols] = g[tm - 8:tm, :]
            tail_ref[:, cols] = g[tm - 8:tm, :]
        else:
            g1 = jnp.where(pos >= 1, g1, 0.0) + s1_ref[:, cols]
            g2 = jnp.where(pos >= 2, g2, 0.0) + s2_ref[:, cols]
            tail_ref[:, cols] = g
        cf = cw_ref[0:1, cols] * g2 + cw_ref[1:2, cols] * g1 + cw_ref[2:3, cols] * g + cbias_ref[:, cols]
        act = cf * _sigmoid(cf) * u
        acc = acc + jnp.dot(act.astype(BF16), wdn_ref[cols, :], preferred_element_type=F32)

    y_ref[...] = _layer_norm(alpha * x + acc, lg_ref[...], lb_ref[...])


def _ffn_call(x, w_up, w_dn, cw, cbias, ln_g, ln_b, layer, tm, seq_len, inject=None):
    n, d = x.shape
    carried = seq_len >= tm
    tiles_per_seq = max(seq_len // tm, 1)
    row = pl.BlockSpec((tm, d), lambda i: (i, 0))
    vec = pl.BlockSpec((None, 1, d), lambda i: (layer, 0, 0))
    once = pl.Buffered(1)
    in_specs = [row]
    args = [x]
    if not carried:
        wide = pl.BlockSpec((tm, D_FF), lambda i: (i, 0))
        in_specs += [wide, wide]
        args += list(inject)
    in_specs += [pl.BlockSpec((None, d, 2 * D_FF), lambda i: (layer, 0, 0), pipeline_mode=once),
                 pl.BlockSpec((None, D_FF, d), lambda i: (layer, 0, 0), pipeline_mode=once),
                 pl.BlockSpec((None, 8, D_FF), lambda i: (layer, 0, 0)),
                 pl.BlockSpec((None, 1, D_FF), lambda i: (layer, 0, 0)), vec, vec]
    args += [w_up, w_dn, cw, cbias, ln_g, ln_b]
    scratch = [pltpu.VMEM((8 + tm, FF_CHUNK), F32)]
    if carried:
        scratch.append(pltpu.VMEM((8, D_FF), F32))
        tail_shape = jax.ShapeDtypeStruct((n // tm, 8, D_FF), F32)
        tail_spec = pl.BlockSpec((None, 8, D_FF), lambda i: (i, 0, 0))
    else:
        tail_shape = jax.ShapeDtypeStruct((n, D_FF), F32)
        tail_spec = pl.BlockSpec((tm, D_FF), lambda i: (i, 0))
    return pl.pallas_call(
        functools.partial(_ffn_body, alpha=_alpha(), tm=tm, seq_len=seq_len, tiles_per_seq=tiles_per_seq),
        grid=(n // tm,), in_specs=in_specs, out_specs=[row, tail_spec],
        out_shape=[jax.ShapeDtypeStruct((n, d), F32), tail_shape],
        scratch_shapes=scratch, compiler_params=_params(1), name="conv_ffn")(*args)


def _pad_rows_front(a, rows):
    return jnp.pad(a, ((0, 0), (0, 0), (rows - a.shape[2], 0), (0, 0)))


def _layer(x, layer, wts, nseq, seq_len, prompt, cfg, sample_in=None):
    (w_in, lam_p, attn_g, cb_w, cb_bias, cb_ng, cb_nb, cc_w, w_br, w_o, ln1_g, ln1_b, w_up, cf_w, cf_bias,
     w_dn, ln2_g, ln2_b) = wts
    d = D_MODEL
    qk_w = N_HEADS * 2 * DK
    o_b0 = 2 * qk_w + N_HEADS * DV
    o_c0 = o_b0 + 2 * d
    o_g0 = o_c0 + 3 * d
    tm, tc = cfg["tm"], cfg["tc"]

    zb, zc, gate_b = _proj_call(x, w_in, layer, [o_b0, o_b0 + d, o_c0, o_c0 + d, o_c0 + 2 * d], d,
                                _bc_epilogue, [F32] * 3, tm, tc, "proj_bc")
    gates = _proj_call(x, w_in, layer, [o_g0, o_g0 + d, o_g0 + 2 * d], d, _gates_epilogue, [F32] * 3,
                       tm, tc, "proj_gates")

    if prompt:
        q, k, v, kb, vb = _proj_call(x, w_in, layer, [0, qk_w, 2 * qk_w], d, _qkv_prompt_epilogue,
                                     [BF16, F32, F32, BF16, BF16], tm, tc, "proj_qkv")
        o_a = _prompt_attention(q, kb, vb, lam_p, attn_g, layer, nseq, seq_len, cfg["tq"], cfg["tk"])
        o_b, o_c = _conv_call(zb, zb, zc, zc, gate_b, cb_w, cb_bias, cb_ng, cb_nb, cc_w, layer,
                              cfg["tt"], seq_len, True, BF16)
    else:
        cache_k, cache_v, page_table, st_b, st_c, st_f = sample_in
        q, k, v = _proj_call(x, w_in, layer, [0, qk_w, 2 * qk_w], d, _qkv_sample_epilogue, [F32] * 3,
                             tm, tc, "proj_qkv")
        rows = seq_len * N_HEADS
        to_rows = lambda a: a.reshape(nseq, rows, DV)
        o_a = _sample_attention(to_rows(q), to_rows(k), to_rows(v), cache_k, cache_v, page_table, lam_p,
                                attn_g, layer, cfg["n_pg"])
        o_b, o_c = _conv_call(zb, st_b[layer].reshape(nseq * HALO_B, d), zc,
                              st_c[layer].reshape(nseq * HALO_C, d), gate_b, cb_w, cb_bias, cb_ng, cb_nb,
                              cc_w, layer, seq_len, seq_len, False, F32)

    x1 = _merge_call(x, o_a, o_b, o_c, gates, w_br, w_o, ln1_g, ln1_b, layer, cfg["tm_merge"])

    if prompt:
        x2, tail = _ffn_call(x1, w_up, w_dn, cf_w, cf_bias, ln2_g, ln2_b, layer, cfg["tm_ffn"], seq_len)
        tps = seq_len // cfg["tm_ffn"]
        new_f = tail.reshape(nseq, tps, 8, D_FF)[:, tps - 1, 8 - (CONV_F - 1):, :]
        new_b = zb.reshape(nseq, seq_len, d)[:, seq_len - (CONV_B - 1):, :]
        new_c = zc.reshape(nseq, seq_len, d)[:, seq_len - (CONV_C - 1):, :]
    else:
        sf = st_f[layer]
        zeros = jnp.zeros((nseq, seq_len, D_FF), F32)
        s1 = zeros.at[:, 0].set(sf[:, 1]).reshape(nseq * seq_len, D_FF)
        s2 = zeros.at[:, 0].set(sf[:, 0]).at[:, 1].set(sf[:, 1]).reshape(nseq * seq_len, D_FF)
        x2, g_f = _ffn_call(x1, w_up, w_dn, cf_w, cf_bias, ln2_g, ln2_b, layer, nseq * seq_len, seq_len,
                            inject=(s1, s2))
        new_f = g_f.reshape(nseq, seq_len, D_FF)[:, seq_len - (CONV_F - 1):, :]
        new_b = jnp.concatenate([st_b[layer][:, HALO_B - (CONV_B - 1) + seq_len:, :],
                                 zb.reshape(nseq, seq_len, d)], axis=1)
        new_c = zc.reshape(nseq, seq_len, d)[:, seq_len - (CONV_C - 1):, :]
    return x2, k, v, new_b, new_c, new_f


PROMPT_CFG = dict(tm=512, tc=512, tq=1024, tk=1024, tt=512, tm_merge=256, tm_ffn=512)
SAMPLE_CFG = dict(tm=256, tc=512, n_pg=16, tm_merge=256)


def kernel(x_prompt, x_sample, cache_k, cache_v, state_conv_b, state_conv_c, state_conv_ffn, page_table, w_in,
           lambda_q1, lambda_k1, lambda_q2, lambda_k2, attn_norm_g, conv_b_w, conv_b_bias, conv_b_norm_g,
           conv_b_norm_b, conv_c_w, w_branch, w_o, ln1_g, ln1_b, w_up, conv_f_w, conv_f_bias, w_down, ln2_g,
           ln2_b):
    nbp, s_len, d = x_prompt.shape
    nd, t_new, _ = x_sample.shape
    depth = w_in.shape[0]
    row3 = lambda a: a.reshape(depth, 1, a.shape[-1])
    pad8 = lambda a: jnp.pad(a, ((0, 0), (0, 8 - a.shape[1] % 8 if a.shape[1] % 8 else 0), (0, 0)))
    wts = (w_in.astype(BF16),
           (row3(lambda_q1), row3(lambda_k1), row3(lambda_q2), row3(lambda_k2)),
           row3(attn_norm_g), pad8(conv_b_w), row3(conv_b_bias), row3(conv_b_norm_g), row3(conv_b_norm_b),
           pad8(conv_c_w), w_branch.astype(BF16), w_o.astype(BF16), row3(ln1_g), row3(ln1_b),
           w_up.astype(BF16), pad8(conv_f_w), row3(conv_f_bias), w_down.astype(BF16), row3(ln2_g), row3(ln2_b))
    pool = cache_k.shape[1]
    ck = cache_k.reshape(depth, pool, PAGE_SIZE * N_HEADS, 2 * DK)
    cv = cache_v.reshape(depth, pool, PAGE_SIZE * N_HEADS, DV)
    sample_in = (ck, cv, page_table, _pad_rows_front(state_conv_b, HALO_B),
                 _pad_rows_front(state_conv_c, HALO_C), state_conv_ffn)

    xp = x_prompt.reshape(nbp * s_len, d)
    xs = x_sample.reshape(nd * t_new, d)
    outs = [[] for _ in range(10)]
    for layer in range(depth):
        xp, k, v, nb_, nc_, nf_ = _layer(xp, layer, wts, nbp, s_len, True, PROMPT_CFG)
        outs[0].append(k.reshape(-1, PAGE_SIZE, N_HEADS, 2 * DK))
        outs[1].append(v.reshape(-1, PAGE_SIZE, N_HEADS, DV))
        outs[2].append(nb_)
        outs[3].append(nc_)
        outs[4].append(nf_)
        xs, k, v, nb_, nc_, nf_ = _layer(xs, layer, wts, nd, t_new, False, SAMPLE_CFG, sample_in)
        outs[5].append(k.reshape(nd, t_new, N_HEADS, 2 * DK))
        outs[6].append(v.reshape(nd, t_new, N_HEADS, DV))
        outs[7].append(nb_)
        outs[8].append(nc_)
        outs[9].append(nf_)
    return (xp.reshape(nbp, s_len, d), xs.reshape(nd, t_new, d)) + tuple(jnp.stack(o) for o in outs)
```
